```python
import math
import jax, jax.numpy as jnp
from jax import lax
import numpy as np

D_MODEL = 1024
BATCH = 8
SEQ = 2048
DEPTH = 4

HEAD_DIM = 64
ATTN_WIDTH = D_MODEL // 2
ATTN_HEADS = ATTN_WIDTH // HEAD_DIM
MOBA_BLOCK = 256
MOBA_TOPK = 3
Q_CHUNK = 64
POOL_WIDTH = D_MODEL // 4
POOL_WINDOWS = (2, 4, 8, 16)
POOL_GROUPS = len(POOL_WINDOWS)
POOL_GROUP_DIM = POOL_WIDTH // POOL_GROUPS
CONV_WIDTH = D_MODEL // 4
CONV_KERNEL = 31
N_BRANCHES = 3
D_FF = -(-8 * D_MODEL // (3 * 256)) * 256
REL_BUCKETS = 32
REL_MAX_DIST = 128
EPS = 1e-6
NEG = -1e30
IN_WIDTH = 3 * ATTN_WIDTH + POOL_WIDTH + 2 * CONV_WIDTH + N_BRANCHES * D_MODEL
IN_SPLITS = [ATTN_WIDTH, 2 * ATTN_WIDTH, 3 * ATTN_WIDTH,
             3 * ATTN_WIDTH + POOL_WIDTH,
             3 * ATTN_WIDTH + POOL_WIDTH + 2 * CONV_WIDTH]

kernel_name = "hybrid_gated_moba_pool_conv_block"


def rms_norm(x, g):
    xf = x.astype(jnp.float32)
    y = xf * lax.rsqrt(jnp.mean(xf * xf, axis=-1, keepdims=True) + EPS)
    return (y * g.astype(jnp.float32)).astype(x.dtype)


def t5_bucket(rel):
    n = jnp.maximum(rel, 0)
    max_exact = REL_BUCKETS // 2
    nf = jnp.maximum(n, 1).astype(jnp.float32)
    large = max_exact + (jnp.log(nf / max_exact) / math.log(REL_MAX_DIST / max_exact)
                         * (REL_BUCKETS - max_exact)).astype(jnp.int32)
    large = jnp.minimum(large, REL_BUCKETS - 1)
    return jnp.where(n < max_exact, n, large)


def moba_attention(q, k, v, rel_bias):
    B, H, S, dh = q.shape
    n_blocks = -(-S // MOBA_BLOCK)
    s_pad = n_blocks * MOBA_BLOCK
    pad = ((0, 0), (0, 0), (0, s_pad - S), (0, 0))
    q, k, v = jnp.pad(q, pad), jnp.pad(k, pad), jnp.pad(v, pad)
    k_blocks = k.reshape(B, H, n_blocks, MOBA_BLOCK, dh)
    v_blocks = v.reshape(B, H, n_blocks, MOBA_BLOCK, dh)
    k_mean = jnp.mean(k_blocks.astype(jnp.float32), axis=3)
    n_sel = min(MOBA_TOPK, n_blocks)
    scale = HEAD_DIM ** -0.5
    table_t = rel_bias.T.astype(jnp.float32)
    b_ix = jnp.arange(B)[:, None, None, None]
    h_ix = jnp.arange(H)[None, :, None, None]
    offs = jnp.arange(MOBA_BLOCK)
    blk_ids = jnp.arange(n_blocks)

    def chunk(start):
        blk = start // MOBA_BLOCK
        qf = lax.dynamic_slice_in_dim(q, start, Q_CHUNK, axis=2).astype(jnp.float32)
        q_pos = start + jnp.arange(Q_CHUNK)
        gate = jnp.einsum('bhqd,bhnd->bhqn', qf, k_mean)
        gate = jnp.where(blk_ids < blk, gate, -jnp.inf)
        _, idx = lax.top_k(gate, n_sel)
        sel_ok = idx < blk
        k_sel = k_blocks[b_ix, h_ix, idx].astype(jnp.float32)
        v_sel = v_blocks[b_ix, h_ix, idx].astype(jnp.float32)
        k_pos_sel = idx[..., None] * MOBA_BLOCK + offs
        bias_sel = table_t[h_ix[..., None], t5_bucket(q_pos[:, None, None] - k_pos_sel)]
        logit_sel = jnp.einsum('bhqd,bhqjkd->bhqjk', qf, k_sel) * scale + bias_sel
        logit_sel = jnp.where(sel_ok[..., None], logit_sel, NEG)
        own_start = blk * MOBA_BLOCK
        k_own = lax.dynamic_slice_in_dim(k, own_start, MOBA_BLOCK, axis=2).astype(jnp.float32)
        v_own = lax.dynamic_slice_in_dim(v, own_start, MOBA_BLOCK, axis=2).astype(jnp.float32)
        rel_own = q_pos[:, None] - (own_start + offs)[None, :]
        bias_own = table_t[:, t5_bucket(rel_own)]
        logit_own = jnp.einsum('bhqd,bhkd->bhqk', qf, k_own) * scale + bias_own
        logit_own = jnp.where(rel_own >= 0, logit_own, NEG)
        logits = jnp.concatenate(
            [logit_sel.reshape(B, H, Q_CHUNK, n_sel * MOBA_BLOCK), logit_own], axis=-1)
        probs = jax.nn.softmax(logits, axis=-1)
        p_sel = probs[..., :n_sel * MOBA_BLOCK].reshape(B, H, Q_CHUNK, n_sel, MOBA_BLOCK)
        p_own = probs[..., n_sel * MOBA_BLOCK:]
        out = (jnp.einsum('bhqjk,bhqjkd->bhqd', p_sel, v_sel)
               + jnp.einsum('bhqk,bhkd->bhqd', p_own, v_own))
        return out.astype(q.dtype)

    starts = jnp.arange(0, s_pad, Q_CHUNK, dtype=jnp.int32)
    outs = lax.map(chunk, starts)
    out = outs.transpose(1, 2, 0, 3, 4).reshape(B, H, s_pad, dh)
    return out[:, :, :S]


def pool_mixer(u, w_group, scale):
    B, S, C = u.shape
    uf = u.astype(jnp.float32)
    cs = jnp.cumsum(uf, axis=1)
    t = jnp.arange(1, S + 1, dtype=jnp.float32)[:, None]
    parts = []
    for g, w in enumerate(POOL_WINDOWS):
        sl = slice(g * POOL_GROUP_DIM, (g + 1) * POOL_GROUP_DIM)
        c = cs[..., sl]
        lag = jnp.pad(c, ((0, 0), (w, 0), (0, 0)))[:, :S]
        parts.append((c - lag) / jnp.minimum(t, float(w)) - uf[..., sl])
    d = jnp.stack(parts, axis=2)
    y = jnp.einsum('bsgc,gcd->bsgd', d, w_group.astype(jnp.float32)).reshape(B, S, C)
    return (y * scale.astype(jnp.float32)).astype(u.dtype)


def conv_module(u_glu, w_dw, b_dw, ln_g, ln_b):
    a, gt = jnp.split(u_glu, 2, axis=-1)
    u = a * jax.nn.sigmoid(gt)
    y = lax.conv_general_dilated(
        u, w_dw[:, None, :].astype(u.dtype), window_strides=(1,),
        padding=[(CONV_KERNEL - 1, 0)], dimension_numbers=('NWC', 'WIO', 'NWC'),
        feature_group_count=CONV_WIDTH) + b_dw
    yf = y.astype(jnp.float32)
    mu = jnp.mean(yf, axis=-1, keepdims=True)
    var = jnp.mean(jnp.square(yf - mu), axis=-1, keepdims=True)
    yn = (yf - mu) * lax.rsqrt(var + EPS) * ln_g + ln_b
    return jax.nn.silu(yn).astype(u.dtype)


def setup_inputs(seed: int = 0) -> dict:
    key = jax.random.key(seed)
    ks = jax.random.split(key, 20)

    def nrm(k, shape, s):
        return jax.random.normal(k, shape, jnp.float32) * s

    gd = POOL_GROUP_DIM
    return {
        "x": nrm(ks[0], (BATCH, SEQ, D_MODEL), 1.0),
        "pre_mix_g": 1.0 + nrm(ks[1], (DEPTH, D_MODEL), 0.1),
        "w_in": nrm(ks[2], (DEPTH, D_MODEL, IN_WIDTH), D_MODEL ** -0.5),
        "b_gate": nrm(ks[3], (DEPTH, N_BRANCHES * D_MODEL), 0.1),
        "w_attn_o": nrm(ks[4], (DEPTH, ATTN_WIDTH, D_MODEL), ATTN_WIDTH ** -0.5),
        "rel_bias": nrm(ks[5], (REL_BUCKETS, ATTN_HEADS), 0.5),
        "w_pool_g": nrm(ks[6], (DEPTH, POOL_GROUPS, gd, gd), gd ** -0.5),
        "pool_scale": 1.0 + nrm(ks[7], (DEPTH, POOL_WIDTH), 0.1),
        "w_pool_o": nrm(ks[8], (DEPTH, POOL_WIDTH, D_MODEL), POOL_WIDTH ** -0.5),
        "w_dw": nrm(ks[9], (DEPTH, CONV_KERNEL, CONV_WIDTH), CONV_KERNEL ** -0.5),
        "b_dw": nrm(ks[10], (DEPTH, CONV_WIDTH), 0.02),
        "conv_ln_g": 1.0 + nrm(ks[11], (DEPTH, CONV_WIDTH), 0.1),
        "conv_ln_b": nrm(ks[12], (DEPTH, CONV_WIDTH), 0.02),
        "w_conv_o": nrm(ks[13], (DEPTH, CONV_WIDTH, D_MODEL), CONV_WIDTH ** -0.5),
        "w_out": nrm(ks[14], (DEPTH, D_MODEL, D_MODEL), D_MODEL ** -0.5),
        "post_mix_g": 1.0 + nrm(ks[15], (DEPTH, D_MODEL), 0.1),
        "pre_ffn_g": 1.0 + nrm(ks[16], (DEPTH, D_MODEL), 0.1),
        "w_ffn_in": nrm(ks[17], (DEPTH, D_MODEL, 2 * D_FF), D_MODEL ** -0.5),
        "w_ffn_out": nrm(ks[18], (DEPTH, D_FF, D_MODEL), D_FF ** -0.5),
        "post_ffn_g": 1.0 + nrm(ks[19], (DEPTH, D_MODEL), 0.1),
    }


def reference(x, pre_mix_g, w_in, b_gate, w_attn_o, rel_bias, w_pool_g, pool_scale,
              w_pool_o, w_dw, b_dw, conv_ln_g, conv_ln_b, w_conv_o, w_out, post_mix_g,
              pre_ffn_g, w_ffn_in, w_ffn_out, post_ffn_g):
    B, S, D = x.shape
    for l in range(DEPTH):
        h = rms_norm(x, pre_mix_g[l])
        proj = h @ w_in[l]
        q, k, v, u_pool, u_conv, g_logits = jnp.split(proj, IN_SPLITS, axis=-1)

        def heads(t):
            return t.reshape(B, S, ATTN_HEADS, HEAD_DIM).transpose(0, 2, 1, 3)

        attn = moba_attention(heads(q), heads(k), heads(v), rel_bias)
        attn = attn.transpose(0, 2, 1, 3).reshape(B, S, ATTN_WIDTH) @ w_attn_o[l]
        pool = pool_mixer(u_pool, w_pool_g[l], pool_scale[l]) @ w_pool_o[l]
        conv = conv_module(u_conv, w_dw[l], b_dw[l], conv_ln_g[l], conv_ln_b[l]) @ w_conv_o[l]
        gates = jax.nn.sigmoid(g_logits + b_gate[l]).reshape(B, S, N_BRANCHES, D)
        merged = gates[:, :, 0] * attn + gates[:, :, 1] * pool + gates[:, :, 2] * conv
        x = x + rms_norm(merged @ w_out[l], post_mix_g[l])
        h = rms_norm(x, pre_ffn_g[l])
        gt, up = jnp.split(h @ w_ffn_in[l], 2, axis=-1)
        f = (jax.nn.silu(gt) * up) @ w_ffn_out[l]
        x = x + rms_norm(f, post_ffn_g[l])
    return x
```

```python
import functools
import math

import jax
import jax.numpy as jnp
from jax import lax
from jax.experimental import pallas as pl
from jax.experimental.pallas import tpu as pltpu

HEAD_DIM = 64
MOBA_BLOCK = 256
MOBA_TOPK = 3
POOL_WINDOWS = (2, 4, 8, 16)
CONV_KERNEL = 31
REL_BUCKETS = 32
REL_MAX_DIST = 128
EPS = 1e-6
NEG = -1e30

LANES = 128
HEADS_PER_STEP = LANES // HEAD_DIM
TOKEN_TILE = 512
VMEM_LIMIT = 56 * 1024 * 1024
POOL_PAD = 16
CONV_PAD = 32
SEQ_TILE = 256

F32 = jnp.float32
BF16 = jnp.bfloat16


def _params(*sem):
    return pltpu.CompilerParams(dimension_semantics=sem, vmem_limit_bytes=VMEM_LIMIT)


def _const_spec(shape, index):
    return pl.BlockSpec(shape, lambda *_: index, pipeline_mode=pl.Buffered(1))


def _rms(x, g):
    return x * lax.rsqrt(jnp.mean(x * x, axis=-1, keepdims=True) + EPS) * g


def _dot(a, b):
    return jnp.dot(a, b, preferred_element_type=F32)


def _inproj_kernel(x_ref, g_ref, w_ref, b_ref, qkv_ref, u_ref, gate_ref, *, attn_w, u_w, chunk):
    h = _rms(x_ref[...], g_ref[...]).astype(BF16)
    qkv_w = 3 * attn_w
    gate_w = gate_ref.shape[1]
    for c in range(0, qkv_w, chunk):
        r = _dot(h, w_ref[:, c:c + chunk])
        if c < attn_w:
            r = r * (HEAD_DIM ** -0.5)
        qkv_ref[:, c:c + chunk] = r.astype(BF16)
    u_ref[...] = _dot(h, w_ref[:, qkv_w:qkv_w + u_w])
    for c in range(0, gate_w, chunk):
        lo = qkv_w + u_w + c
        r = _dot(h, w_ref[:, lo:lo + chunk]) + b_ref[:, c:c + chunk]
        gate_ref[:, c:c + chunk] = jax.nn.sigmoid(r)


def _inproj(x2, g, w, b, l, attn_w, u_w):
    m, d = x2.shape
    in_w = w.shape[2]
    gate_w = in_w - 3 * attn_w - u_w
    kern = functools.partial(_inproj_kernel, attn_w=attn_w, u_w=u_w, chunk=512)
    return pl.pallas_call(
        kern,
        grid=(m // TOKEN_TILE,),
        in_specs=[
            pl.BlockSpec((TOKEN_TILE, d), lambda i: (i, 0)),
            _const_spec((None, 1, d), (l, 0, 0)),
            _const_spec((None, d, in_w), (l, 0, 0)),
            _const_spec((None, 1, gate_w), (l, 0, 0)),
        ],
        out_specs=[
            pl.BlockSpec((TOKEN_TILE, 3 * attn_w), lambda i: (i, 0)),
            pl.BlockSpec((TOKEN_TILE, u_w), lambda i: (i, 0)),
            pl.BlockSpec((TOKEN_TILE, gate_w), lambda i: (i, 0)),
        ],
        out_shape=[
            jax.ShapeDtypeStruct((m, 3 * attn_w), BF16),
            jax.ShapeDtypeStruct((m, u_w), F32),
            jax.ShapeDtypeStruct((m, gate_w), F32),
        ],
        compiler_params=_params("arbitrary"),
        name="inproj",
    )(x2, g, w, b)


def _moba_kernel(q_ref, k_ref, v_ref, b0_ref, b1_ref, cfar_ref, o_ref,
                 kmean_ref, vt_ref, sel_ref, *, n_blocks):
    qi = pl.program_id(2)
    blk = MOBA_BLOCK
    vrows = vt_ref.shape[1]

    @pl.when(qi == 0)
    def _prepare():
        for j in range(n_blocks):
            kb = k_ref[0, j * blk:(j + 1) * blk, :].astype(F32)
            kmean_ref[j:j + 1, :] = jnp.mean(kb, axis=0, keepdims=True)
            vt = v_ref[0, j * blk:(j + 1) * blk, :].astype(F32).T
            for h in range(HEADS_PER_STEP):
                vt_ref[h, 0:HEAD_DIM, j * blk:(j + 1) * blk] = (
                    vt[h * HEAD_DIM:(h + 1) * HEAD_DIM].astype(BF16))
        for h in range(HEADS_PER_STEP):
            vt_ref[h, HEAD_DIM:vrows, :] = jnp.ones((vrows - HEAD_DIM, vt_ref.shape[2]), BF16)

    qt = q_ref[0].astype(F32).T
    dim_row = lax.broadcasted_iota(jnp.int32, qt.shape, 0)
    blk_row = lax.broadcasted_iota(jnp.int32, (n_blocks, blk), 0)
    outs = []
    for h in range(HEADS_PER_STEP):
        in_head = (dim_row >= h * HEAD_DIM) & (dim_row < (h + 1) * HEAD_DIM)
        qt_h = jnp.where(in_head, qt, 0.0)
        qt_hb = qt_h.astype(BF16)

        gate = jnp.dot(kmean_ref[...], qt_h, preferred_element_type=F32,
                       precision=lax.Precision.HIGHEST)
        rank = jnp.zeros((n_blocks, blk), jnp.int32)
        for j in range(n_blocks):
            other = gate[j:j + 1, :]
            beats = (other > gate) | ((other == gate) & (j < blk_row))
            rank = rank + jnp.where(beats & (j < qi), 1, 0)
        sel = (blk_row < qi) & (rank < MOBA_TOPK)
        sel_ref[...] = jnp.where(sel, 1.0, 0.0)

        def scores(j):
            start = pl.multiple_of(j * blk, blk)
            return _dot(k_ref[0, pl.ds(start, blk), :], qt_hb), start

        def past_tile(j, st, bias_const, m, acc):
            st_max = jnp.max(st, axis=0, keepdims=True) + bias_const
            chosen = sel_ref[pl.ds(j, 1), :]
            m_new = jnp.maximum(m, jnp.where(chosen > 0.0, st_max, NEG))
            m_use = jnp.maximum(m_new, st_max)
            p = jnp.exp(st - (m_use - bias_const)).astype(BF16)
            start = pl.multiple_of(j * blk, blk)
            pv = _dot(vt_ref[h, :, pl.ds(start, blk)], p)
            acc = acc * jnp.exp(m - m_new) + pv * chosen
            return m_new, acc

        cfar = cfar_ref[h, 0:1, 0:1]

        def far_body(j, carry):
            st, _ = scores(j)
            return past_tile(j, st, cfar, *carry)

        m = jnp.full((1, blk), NEG, F32)
        acc = jnp.zeros((vrows, blk), F32)
        m, acc = lax.fori_loop(0, jnp.maximum(qi - 1, 0), far_body, (m, acc))

        jn = jnp.maximum(qi - 1, 0)
        st, _ = scores(jn)
        m, acc = past_tile(jn, st + b1_ref[h], 0.0, m, acc)

        st, start = scores(qi)
        st = st + b0_ref[h]
        m_new = jnp.maximum(m, jnp.max(st, axis=0, keepdims=True))
        p = jnp.exp(st - m_new).astype(BF16)
        pv = _dot(vt_ref[h, :, pl.ds(start, blk)], p)
        acc = acc * jnp.exp(m - m_new) + pv
        outs.append(acc[0:HEAD_DIM] / acc[HEAD_DIM:HEAD_DIM + 1])

    o_ref[0] = jnp.concatenate(outs, axis=0).T.astype(o_ref.dtype)


def _moba(qkv3, bias0_t, bias1_t, cfar, attn_w):
    b, s, _ = qkv3.shape
    n_heads = attn_w // HEAD_DIM
    n_blocks = s // MOBA_BLOCK
    hp = n_heads // HEADS_PER_STEP
    cols = attn_w // LANES
    vrows = HEAD_DIM + 16
    kern = functools.partial(_moba_kernel, n_blocks=n_blocks)
    return pl.pallas_call(
        kern,
        grid=(hp, b, n_blocks),
        in_specs=[
            pl.BlockSpec((1, MOBA_BLOCK, LANES), lambda p, i, q: (i, q, p)),
            pl.BlockSpec((1, s, LANES), lambda p, i, q: (i, 0, cols + p)),
            pl.BlockSpec((1, s, LANES), lambda p, i, q: (i, 0, 2 * cols + p)),
            pl.BlockSpec((HEADS_PER_STEP, MOBA_BLOCK, MOBA_BLOCK), lambda p, i, q: (p, 0, 0)),
            pl.BlockSpec((HEADS_PER_STEP, MOBA_BLOCK, MOBA_BLOCK), lambda p, i, q: (p, 0, 0)),
            pl.BlockSpec((HEADS_PER_STEP, 8, LANES), lambda p, i, q: (p, 0, 0)),
        ],
        out_specs=pl.BlockSpec((1, MOBA_BLOCK, LANES), lambda p, i, q: (i, q, p)),
        out_shape=jax.ShapeDtypeStruct((b, s, attn_w), BF16),
        scratch_shapes=[
            pltpu.VMEM((n_blocks, LANES), F32),
            pltpu.VMEM((HEADS_PER_STEP, vrows, s), BF16),
            pltpu.VMEM((n_blocks, MOBA_BLOCK), F32),
        ],
        compiler_params=_params("arbitrary", "arbitrary", "arbitrary"),
        name="moba",
    )(qkv3, qkv3, qkv3, bias0_t, bias1_t, cfar)


def _t5_bucket(rel):
    n = jnp.maximum(rel, 0)
    max_exact = REL_BUCKETS // 2
    nf = jnp.maximum(n, 1).astype(F32)
    large = max_exact + (jnp.log(nf / max_exact) / math.log(REL_MAX_DIST / max_exact)
                         * (REL_BUCKETS - max_exact)).astype(jnp.int32)
    large = jnp.minimum(large, REL_BUCKETS - 1)
    return jnp.where(n < max_exact, n, large)


def _bias_tables(rel_bias):
    table_t = rel_bias.T.astype(F32)
    offs = jnp.arange(MOBA_BLOCK)
    rel_own = offs[None, :] - offs[:, None]
    bias0 = jnp.where(rel_own >= 0, table_t[:, _t5_bucket(rel_own)], NEG)
    bias1 = table_t[:, _t5_bucket(rel_own + MOBA_BLOCK)]
    far = _t5_bucket(jnp.int32(2 * MOBA_BLOCK - (MOBA_BLOCK - 1)))
    cfar = jnp.broadcast_to(table_t[:, far][:, None, None], (table_t.shape[0], 8, LANES))
    return bias0, bias1, cfar


def _seq_kernel(u_ref, wg_ref, ps_ref, wdw_ref, bdw_ref, lng_ref, lnb_ref,
                pool_ref, conv_ref, ppad_ref, cpad_ref, *, pool_w, conv_w):
    s = u_ref.shape[1]
    ppad_ref[0:POOL_PAD, :] = jnp.zeros((POOL_PAD, pool_w), F32)
    cpad_ref[0:CONV_PAD, :] = jnp.zeros((CONV_PAD, conv_w), F32)
    group_w = pool_w // len(POOL_WINDOWS)

    for t0 in range(0, s, SEQ_TILE):
        rows = pl.ds(t0, SEQ_TILE)
        ppad_ref[pl.ds(POOL_PAD + t0, SEQ_TILE), :] = u_ref[0, rows, 0:pool_w]
        a = u_ref[0, rows, pool_w:pool_w + conv_w]
        gt = u_ref[0, rows, pool_w + conv_w:pool_w + 2 * conv_w]
        cpad_ref[pl.ds(CONV_PAD + t0, SEQ_TILE), :] = a * jax.nn.sigmoid(gt)

    lane_group = lax.broadcasted_iota(jnp.int32, (SEQ_TILE, pool_w), 1) // group_w
    window = jnp.zeros((SEQ_TILE, pool_w), F32)
    for gi, w in enumerate(POOL_WINDOWS):
        window = jnp.where(lane_group == gi, float(w), window)
    row = lax.broadcasted_iota(jnp.int32, (SEQ_TILE, pool_w), 0)

    for t0 in range(0, s, SEQ_TILE):
        tok = ppad_ref[pl.ds(POOL_PAD + t0, SEQ_TILE), :]
        run = tok
        wsum = jnp.zeros_like(tok)
        for k in range(1, max(POOL_WINDOWS)):
            run = run + ppad_ref[pl.ds(POOL_PAD + t0 - k, SEQ_TILE), :]
            if k + 1 in POOL_WINDOWS:
                wsum = jnp.where(window == float(k + 1), run, wsum)
        count = jnp.minimum((row + t0 + 1).astype(F32), window)
        d = wsum / count - tok
        y = _dot(d.astype(BF16), wg_ref[...]) * ps_ref[...]
        pool_ref[0, pl.ds(t0, SEQ_TILE), :] = y.astype(pool_ref.dtype)
        acc = jnp.zeros((SEQ_TILE, conv_w), F32) + bdw_ref[...]
        for k in range(CONV_KERNEL):
            off = CONV_PAD + t0 - (CONV_KERNEL - 1) + k
            acc = acc + cpad_ref[pl.ds(off, SEQ_TILE), :] * wdw_ref[k:k + 1, :]
        mu = jnp.mean(acc, axis=-1, keepdims=True)
        cen = acc - mu
        var = jnp.mean(cen * cen, axis=-1, keepdims=True)
        yn = cen * lax.rsqrt(var + EPS) * lng_ref[...] + lnb_ref[...]
        conv_ref[0, pl.ds(t0, SEQ_TILE), :] = (yn * jax.nn.sigmoid(yn)).astype(conv_ref.dtype)


def _seq_mixers(u3, wg_bd, pool_scale, w_dw, b_dw, ln_g, ln_b, l, pool_w, conv_w):
    b, s, u_w = u3.shape
    kern = functools.partial(_seq_kernel, pool_w=pool_w, conv_w=conv_w)
    return pl.pallas_call(
        kern,
        grid=(b,),
        in_specs=[
            pl.BlockSpec((1, s, u_w), lambda i: (i, 0, 0)),
            _const_spec((None, pool_w, pool_w), (l, 0, 0)),
            _const_spec((None, 1, pool_w), (l, 0, 0)),
            _const_spec((None, CONV_KERNEL, conv_w), (l, 0, 0)),
            _const_spec((None, 1, conv_w), (l, 0, 0)),
            _const_spec((None, 1, conv_w), (l, 0, 0)),
            _const_spec((None, 1, conv_w), (l, 0, 0)),
        ],
        out_specs=[
            pl.BlockSpec((1, s, pool_w), lambda i: (i, 0, 0)),
            pl.BlockSpec((1, s, conv_w), lambda i: (i, 0, 0)),
        ],
        out_shape=[
            jax.ShapeDtypeStruct((b, s, pool_w), BF16),
            jax.ShapeDtypeStruct((b, s, conv_w), BF16),
        ],
        scratch_shapes=[
            pltpu.VMEM((POOL_PAD + s, pool_w), F32),
            pltpu.VMEM((CONV_PAD + s, conv_w), F32),
        ],
        compiler_params=_params("arbitrary"),
        name="seq_mixers",
    )(u3, wg_bd, pool_scale, w_dw, b_dw, ln_g, ln_b)


def _merge_kernel(x_ref, attn_ref, pool_ref, conv_ref, gate_ref, wa_ref, wp_ref, wc_ref,
                  wo_ref, g_ref, o_ref):
    d = x_ref.shape[1]
    merged = gate_ref[:, 0:d] * _dot(attn_ref[...], wa_ref[...])
    merged = merged + gate_ref[:, d:2 * d] * _dot(pool_ref[...], wp_ref[...])
    merged = merged + gate_ref[:, 2 * d:3 * d] * _dot(conv_ref[...], wc_ref[...])
    y = _dot(merged.astype(BF16), wo_ref[...])
    o_ref[...] = x_ref[...] + _rms(y, g_ref[...])


def _merge(x2, attn, pool, conv, gates, wa, wp, wc, wo, g, l):
    m, d = x2.shape
    tile = lambda w: pl.BlockSpec((TOKEN_TILE, w), lambda i: (i, 0))
    return pl.pallas_call(
        _merge_kernel,
        grid=(m // TOKEN_TILE,),
        in_specs=[
            tile(d), tile(attn.shape[1]), tile(pool.shape[1]), tile(conv.shape[1]),
            tile(gates.shape[1]),
            _const_spec((None,) + wa.shape[1:], (l, 0, 0)),
            _const_spec((None,) + wp.shape[1:], (l, 0, 0)),
            _const_spec((None,) + wc.shape[1:], (l, 0, 0)),
            _const_spec((None,) + wo.shape[1:], (l, 0, 0)),
            _const_spec((None, 1, d), (l, 0, 0)),
        ],
        out_specs=tile(d),
        out_shape=jax.ShapeDtypeStruct((m, d), F32),
        compiler_params=_params("arbitrary"),
        name="merge",
    )(x2, attn, pool, conv, gates, wa, wp, wc, wo, g)


def _ffn_kernel(x_ref, gpre_ref, win_ref, wout_ref, gpost_ref, o_ref, *, d_ff, chunk):
    x = x_ref[...]
    h = _rms(x, gpre_ref[...]).astype(BF16)
    f = None
    for c in range(0, d_ff, chunk):
        gt = _dot(h, win_ref[:, c:c + chunk])
        up = _dot(h, win_ref[:, d_ff + c:d_ff + c + chunk])
        act = (gt * jax.nn.sigmoid(gt) * up).astype(BF16)
        part = _dot(act, wout_ref[c:c + chunk, :])
        f = part if f is None else f + part
    o_ref[...] = x + _rms(f, gpost_ref[...])


def _ffn(x2, gpre, win, wout, gpost, l):
    m, d = x2.shape
    d_ff = wout.shape[1]
    kern = functools.partial(_ffn_kernel, d_ff=d_ff, chunk=d_ff // 2)
    return pl.pallas_call(
        kern,
        grid=(m // TOKEN_TILE,),
        in_specs=[
            pl.BlockSpec((TOKEN_TILE, d), lambda i: (i, 0)),
            _const_spec((None, 1, d), (l, 0, 0)),
            _const_spec((None, d, 2 * d_ff), (l, 0, 0)),
            _const_spec((None, d_ff, d), (l, 0, 0)),
            _const_spec((None, 1, d), (l, 0, 0)),
        ],
        out_specs=pl.BlockSpec((TOKEN_TILE, d), lambda i: (i, 0)),
        out_shape=jax.ShapeDtypeStruct((m, d), F32),
        compiler_params=_params("arbitrary"),
        name="ffn",
    )(x2, gpre, win, wout, gpost)


def _block_diag(w):
    depth, groups, gd, _ = w.shape
    eye = jnp.eye(groups, dtype=w.dtype)
    return jnp.einsum('lgcd,gh->lgchd', w, eye).reshape(depth, groups * gd, groups * gd)


def kernel(x, pre_mix_g, w_in, b_gate, w_attn_o, rel_bias, w_pool_g, pool_scale, w_pool_o, w_dw,
           b_dw, conv_ln_g, conv_ln_b, w_conv_o, w_out, post_mix_g, pre_ffn_g, w_ffn_in,
           w_ffn_out, post_ffn_g):
    b, s, d = x.shape
    depth = w_in.shape[0]
    attn_w = w_attn_o.shape[1]
    pool_w = w_pool_o.shape[1]
    conv_w = w_conv_o.shape[1]
    u_w = pool_w + 2 * conv_w

    row = lambda a: a.reshape(depth, 1, a.shape[-1])
    w_in_b = w_in.astype(BF16)
    wa_b, wp_b, wc_b, wo_b = (w.astype(BF16) for w in (w_attn_o, w_pool_o, w_conv_o, w_out))
    wfi_b, wfo_b = w_ffn_in.astype(BF16), w_ffn_out.astype(BF16)
    wg_bd = _block_diag(w_pool_g).astype(BF16)
    bias0_t, bias1_t, cfar = _bias_tables(rel_bias)

    x2 = x.reshape(b * s, d)
    for l in range(depth):
        qkv, u, gates = _inproj(x2, row(pre_mix_g), w_in_b, row(b_gate), l, attn_w, u_w)
        attn = _moba(qkv.reshape(b, s, 3 * attn_w), bias0_t, bias1_t, cfar, attn_w)
        pool, conv = _seq_mixers(u.reshape(b, s, u_w), wg_bd, row(pool_scale), w_dw, row(b_dw),
                                 row(conv_ln_g), row(conv_ln_b), l, pool_w, conv_w)
        x2 = _merge(x2, attn.reshape(b * s, attn_w), pool.reshape(b * s, pool_w),
                    conv.reshape(b * s, conv_w), gates, wa_b, wp_b, wc_b, wo_b, row(post_mix_g), l)
        x2 = _ffn(x2, row(pre_ffn_g), wfi_b, wfo_b, row(post_ffn_g), l)
    return x2.reshape(b, s, d)
```

```python
import functools
import math

import jax
import jax.numpy as jnp
from jax import lax
from jax.experimental import pallas as pl
from jax.experimental.pallas import tpu as pltpu

HEAD_DIM = 64
MOBA_BLOCK = 256
MOBA_TOPK = 3
POOL_WINDOWS = (2, 4, 8, 16)
CONV_KERNEL = 31
REL_BUCKETS = 32
REL_MAX_DIST = 128
EPS = 1e-6
NEG = -1e30

LANES = 128
HEADS_PER_STEP = LANES // HEAD_DIM
TOKEN_TILE = 512
VMEM_LIMIT = 56 * 1024 * 1024
POOL_PAD = 16
CONV_PAD = 32
SEQ_TILE = 256
ONES_ROWS = 16

F32 = jnp.float32
BF16 = jnp.bfloat16


def _params(*sem):
    return pltpu.CompilerParams(dimension_semantics=sem, vmem_limit_bytes=VMEM_LIMIT)


def _const_spec(shape, index):
    return pl.BlockSpec(shape, lambda *_: index, pipeline_mode=pl.Buffered(1))


def _rms(x, g):
    return x * lax.rsqrt(jnp.mean(x * x, axis=-1, keepdims=True) + EPS) * g


def _dot(a, b):
    return jnp.dot(a, b, preferred_element_type=F32)


def _inproj_kernel(x_ref, g_ref, w_ref, b_ref, qkv_ref, u_ref, gate_ref, *, attn_w, u_w, chunk):
    h = _rms(x_ref[...], g_ref[...]).astype(BF16)
    qkv_w = 3 * attn_w
    gate_w = gate_ref.shape[1]
    for c in range(0, qkv_w, chunk):
        r = _dot(h, w_ref[:, c:c + chunk])
        if c < attn_w:
            r = r * (HEAD_DIM ** -0.5)
        qkv_ref[:, c:c + chunk] = r.astype(BF16)
    u_ref[...] = _dot(h, w_ref[:, qkv_w:qkv_w + u_w])
    for c in range(0, gate_w, chunk):
        lo = qkv_w + u_w + c
        r = _dot(h, w_ref[:, lo:lo + chunk]) + b_ref[:, c:c + chunk]
        gate_ref[:, c:c + chunk] = jax.nn.sigmoid(r)


def _inproj(x2, g, w, b, l, attn_w, u_w):
    m, d = x2.shape
    in_w = w.shape[2]
    gate_w = in_w - 3 * attn_w - u_w
    kern = functools.partial(_inproj_kernel, attn_w=attn_w, u_w=u_w, chunk=512)
    return pl.pallas_call(
        kern,
        grid=(m // TOKEN_TILE,),
        in_specs=[
            pl.BlockSpec((TOKEN_TILE, d), lambda i: (i, 0)),
            _const_spec((None, 1, d), (l, 0, 0)),
            _const_spec((None, d, in_w), (l, 0, 0)),
            _const_spec((None, 1, gate_w), (l, 0, 0)),
        ],
        out_specs=[
            pl.BlockSpec((TOKEN_TILE, 3 * attn_w), lambda i: (i, 0)),
            pl.BlockSpec((TOKEN_TILE, u_w), lambda i: (i, 0)),
            pl.BlockSpec((TOKEN_TILE, gate_w), lambda i: (i, 0)),
        ],
        out_shape=[
            jax.ShapeDtypeStruct((m, 3 * attn_w), BF16),
            jax.ShapeDtypeStruct((m, u_w), F32),
            jax.ShapeDtypeStruct((m, gate_w), F32),
        ],
        compiler_params=_params("arbitrary"),
        name="inproj",
    )(x2, g, w, b)


def _moba_kernel(q_ref, k_ref, v_ref, b0_ref, b1_ref, cfar_ref, o_ref,
                 kmean_ref, vt_ref, s_ref, p_ref, *, n_blocks):
    blk = MOBA_BLOCK
    for j in range(n_blocks):
        kb = k_ref[0, j * blk:(j + 1) * blk, :].astype(F32)
        kmean_ref[j:j + 1, :] = jnp.mean(kb, axis=0, keepdims=True)
        vt = v_ref[0, j * blk:(j + 1) * blk, :].astype(F32).T
        for h in range(HEADS_PER_STEP):
            vt_ref[h, 0:HEAD_DIM, j * blk:(j + 1) * blk] = (
                vt[h * HEAD_DIM:(h + 1) * HEAD_DIM].astype(BF16))
    for h in range(HEADS_PER_STEP):
        vt_ref[h, HEAD_DIM:HEAD_DIM + ONES_ROWS, :] = jnp.ones((ONES_ROWS, vt_ref.shape[2]), BF16)

    dim_row = lax.broadcasted_iota(jnp.int32, (LANES, blk), 0)
    blk_row = lax.broadcasted_iota(jnp.int32, (n_blocks, blk), 0)
    kmean = kmean_ref[...]

    for qi in range(n_blocks):
        qt = q_ref[0, qi * blk:(qi + 1) * blk, :].astype(F32).T
        outs = []
        for h in range(HEADS_PER_STEP):
            buf = (qi % 2) * HEADS_PER_STEP + h
            in_head = (dim_row >= h * HEAD_DIM) & (dim_row < (h + 1) * HEAD_DIM)
            qt_h = jnp.where(in_head, qt, 0.0)
            qt_hb = qt_h.astype(BF16)

            gate = jnp.dot(kmean, qt_h, preferred_element_type=F32,
                           precision=lax.Precision.HIGHEST)
            rank = jnp.zeros((n_blocks, blk), jnp.int32)
            for j in range(qi):
                other = gate[j:j + 1, :]
                beats = (other > gate) | ((other == gate) & (j < blk_row))
                rank = rank + jnp.where(beats, 1, 0)
            sel = (blk_row < qi) & (rank < MOBA_TOPK)

            cfar = cfar_ref[h, 0:1, 0:1]
            col_max, col_bias = [], []
            n_far = max(qi - 1, 0)
            if n_far:
                r = _dot(k_ref[0, 0:n_far * blk, :], qt_hb)
                s_ref[buf, 0:n_far * blk, :] = r
                for j in range(n_far):
                    col_max.append(jnp.max(r[j * blk:(j + 1) * blk], axis=0, keepdims=True) + cfar)
                    col_bias.append(cfar)
            if qi >= 1:
                lo = (qi - 1) * blk
                r = _dot(k_ref[0, lo:lo + blk, :], qt_hb) + b1_ref[h]
                s_ref[buf, lo:lo + blk, :] = r
                col_max.append(jnp.max(r, axis=0, keepdims=True))
                col_bias.append(None)
            lo = qi * blk
            r = _dot(k_ref[0, lo:lo + blk, :], qt_hb) + b0_ref[h]
            s_ref[buf, lo:lo + blk, :] = r
            m = jnp.max(r, axis=0, keepdims=True)
            chosen = [sel[j:j + 1, :] for j in range(qi)]
            for j in range(qi):
                m = jnp.maximum(m, jnp.where(chosen[j], col_max[j], NEG))

            for j in range(qi + 1):
                if j == qi:
                    off = m
                else:
                    base = m if col_bias[j] is None else m - col_bias[j]
                    off = jnp.where(chosen[j], base, -NEG)
                p = jnp.exp(s_ref[buf, j * blk:(j + 1) * blk, :] - off)
                p_ref[buf, j * blk:(j + 1) * blk, :] = p.astype(BF16)
            n_keys = (qi + 1) * blk
            acc = _dot(vt_ref[h, :, 0:n_keys], p_ref[buf, 0:n_keys, :])
            outs.append(acc[0:HEAD_DIM] / acc[HEAD_DIM:HEAD_DIM + 1])

        o_ref[0, qi * blk:(qi + 1) * blk, :] = jnp.concatenate(outs, axis=0).T.astype(o_ref.dtype)


def _moba(qkv3, bias0_t, bias1_t, cfar, attn_w):
    b, s, _ = qkv3.shape
    n_heads = attn_w // HEAD_DIM
    n_blocks = s // MOBA_BLOCK
    hp = n_heads // HEADS_PER_STEP
    cols = attn_w // LANES
    n_buf = 2 * HEADS_PER_STEP
    kern = functools.partial(_moba_kernel, n_blocks=n_blocks)
    bias_spec = pl.BlockSpec((HEADS_PER_STEP, MOBA_BLOCK, MOBA_BLOCK), lambda p, i: (p, 0, 0))
    return pl.pallas_call(
        kern,
        grid=(hp, b),
        in_specs=[
            pl.BlockSpec((1, s, LANES), lambda p, i: (i, 0, p)),
            pl.BlockSpec((1, s, LANES), lambda p, i: (i, 0, cols + p)),
            pl.BlockSpec((1, s, LANES), lambda p, i: (i, 0, 2 * cols + p)),
            bias_spec,
            bias_spec,
            pl.BlockSpec((HEADS_PER_STEP, 8, LANES), lambda p, i: (p, 0, 0)),
        ],
        out_specs=pl.BlockSpec((1, s, LANES), lambda p, i: (i, 0, p)),
        out_shape=jax.ShapeDtypeStruct((b, s, attn_w), BF16),
        scratch_shapes=[
            pltpu.VMEM((n_blocks, LANES), F32),
            pltpu.VMEM((HEADS_PER_STEP, HEAD_DIM + ONES_ROWS, s), BF16),
            pltpu.VMEM((n_buf, s, MOBA_BLOCK), F32),
            pltpu.VMEM((n_buf, s, MOBA_BLOCK), BF16),
        ],
        compiler_params=_params("arbitrary", "arbitrary"),
        name="moba",
    )(qkv3, qkv3, qkv3, bias0_t, bias1_t, cfar)


def _t5_bucket(rel):
    n = jnp.maximum(rel, 0)
    max_exact = REL_BUCKETS // 2
    nf = jnp.maximum(n, 1).astype(F32)
    large = max_exact + (jnp.log(nf / max_exact) / math.log(REL_MAX_DIST / max_exact)
                         * (REL_BUCKETS - max_exact)).astype(jnp.int32)
    large = jnp.minimum(large, REL_BUCKETS - 1)
    return jnp.where(n < max_exact, n, large)


def _bias_tables(rel_bias):
    table_t = rel_bias.T.astype(F32)
    offs = jnp.arange(MOBA_BLOCK)
    rel_own = offs[None, :] - offs[:, None]
    buckets = jnp.arange(REL_BUCKETS)[:, None, None]

    def lookup(rel):
        onehot = (_t5_bucket(rel)[None] == buckets).astype(F32)
        return jnp.einsum('hb,bkq->hkq', table_t, onehot, precision=lax.Precision.HIGHEST)

    bias0 = jnp.where(rel_own >= 0, lookup(rel_own), NEG)
    bias1 = lookup(rel_own + MOBA_BLOCK)
    far = _t5_bucket(jnp.int32(2 * MOBA_BLOCK - (MOBA_BLOCK - 1)))
    cfar = jnp.broadcast_to(table_t[:, far][:, None, None], (table_t.shape[0], 8, LANES))
    return bias0, bias1, cfar


def _seq_kernel(u_ref, wg_ref, ps_ref, wdw_ref, bdw_ref, lng_ref, lnb_ref,
                pool_ref, conv_ref, ppad_ref, cpad_ref, *, pool_w, conv_w):
    s = u_ref.shape[1]
    ppad_ref[0:POOL_PAD, :] = jnp.zeros((POOL_PAD, pool_w), F32)
    cpad_ref[0:CONV_PAD, :] = jnp.zeros((CONV_PAD, conv_w), F32)
    group_w = pool_w // len(POOL_WINDOWS)

    for t0 in range(0, s, SEQ_TILE):
        rows = pl.ds(t0, SEQ_TILE)
        ppad_ref[pl.ds(POOL_PAD + t0, SEQ_TILE), :] = u_ref[0, rows, 0:pool_w]
        a = u_ref[0, rows, pool_w:pool_w + conv_w]
        gt = u_ref[0, rows, pool_w + conv_w:pool_w + 2 * conv_w]
        cpad_ref[pl.ds(CONV_PAD + t0, SEQ_TILE), :] = a * jax.nn.sigmoid(gt)

    lane_group = lax.broadcasted_iota(jnp.int32, (SEQ_TILE, pool_w), 1) // group_w
    window = jnp.zeros((SEQ_TILE, pool_w), F32)
    for gi, w in enumerate(POOL_WINDOWS):
        window = jnp.where(lane_group == gi, float(w), window)
    row = lax.broadcasted_iota(jnp.int32, (SEQ_TILE, pool_w), 0)

    for t0 in range(0, s, SEQ_TILE):
        tok = ppad_ref[pl.ds(POOL_PAD + t0, SEQ_TILE), :]
        run = tok
        wsum = jnp.zeros_like(tok)
        for k in range(1, max(POOL_WINDOWS)):
            run = run + ppad_ref[pl.ds(POOL_PAD + t0 - k, SEQ_TILE), :]
            if k + 1 in POOL_WINDOWS:
                wsum = jnp.where(window == float(k + 1), run, wsum)
        count = jnp.minimum((row + t0 + 1).astype(F32), window)
        d = wsum / count - tok
        y = _dot(d.astype(BF16), wg_ref[...]) * ps_ref[...]
        pool_ref[0, pl.ds(t0, SEQ_TILE), :] = y.astype(pool_ref.dtype)
        acc = jnp.zeros((SEQ_TILE, conv_w), F32) + bdw_ref[...]
        for k in range(CONV_KERNEL):
            off = CONV_PAD + t0 - (CONV_KERNEL - 1) + k
            acc = acc + cpad_ref[pl.ds(off, SEQ_TILE), :] * wdw_ref[k:k + 1, :]
        mu = jnp.mean(acc, axis=-1, keepdims=True)
        cen = acc - mu
        var = jnp.mean(cen * cen, axis=-1, keepdims=True)
        yn = cen * lax.rsqrt(var + EPS) * lng_ref[...] + lnb_ref[...]
        conv_ref[0, pl.ds(t0, SEQ_TILE), :] = (yn * jax.nn.sigmoid(yn)).astype(conv_ref.dtype)


def _seq_mixers(u3, wg_bd, pool_scale, w_dw, b_dw, ln_g, ln_b, l, pool_w, conv_w):
    b, s, u_w = u3.shape
    kern = functools.partial(_seq_kernel, pool_w=pool_w, conv_w=conv_w)
    return pl.pallas_call(
        kern,
        grid=(b,),
        in_specs=[
            pl.BlockSpec((1, s, u_w), lambda i: (i, 0, 0)),
            _const_spec((None, pool_w, pool_w), (l, 0, 0)),
            _const_spec((None, 1, pool_w), (l, 0, 0)),
            _const_spec((None, CONV_KERNEL, conv_w), (l, 0, 0)),
            _const_spec((None, 1, conv_w), (l, 0, 0)),
            _const_spec((None, 1, conv_w), (l, 0, 0)),
            _const_spec((None, 1, conv_w), (l, 0, 0)),
        ],
        out_specs=[
            pl.BlockSpec((1, s, pool_w), lambda i: (i, 0, 0)),
            pl.BlockSpec((1, s, conv_w), lambda i: (i, 0, 0)),
        ],
        out_shape=[
            jax.ShapeDtypeStruct((b, s, pool_w), BF16),
            jax.ShapeDtypeStruct((b, s, conv_w), BF16),
        ],
        scratch_shapes=[
            pltpu.VMEM((POOL_PAD + s, pool_w), F32),
            pltpu.VMEM((CONV_PAD + s, conv_w), F32),
        ],
        compiler_params=_params("arbitrary"),
        name="seq_mixers",
    )(u3, wg_bd, pool_scale, w_dw, b_dw, ln_g, ln_b)


def _merge_kernel(x_ref, attn_ref, pool_ref, conv_ref, gate_ref, wa_ref, wp_ref, wc_ref,
                  wo_ref, g_ref, o_ref):
    d = x_ref.shape[1]
    merged = gate_ref[:, 0:d] * _dot(attn_ref[...], wa_ref[...])
    merged = merged + gate_ref[:, d:2 * d] * _dot(pool_ref[...], wp_ref[...])
    merged = merged + gate_ref[:, 2 * d:3 * d] * _dot(conv_ref[...], wc_ref[...])
    y = _dot(merged.astype(BF16), wo_ref[...])
    o_ref[...] = x_ref[...] + _rms(y, g_ref[...])


def _merge(x2, attn, pool, conv, gates, wa, wp, wc, wo, g, l):
    m, d = x2.shape
    tile = lambda w: pl.BlockSpec((TOKEN_TILE, w), lambda i: (i, 0))
    return pl.pallas_call(
        _merge_kernel,
        grid=(m // TOKEN_TILE,),
        in_specs=[
            tile(d), tile(attn.shape[1]), tile(pool.shape[1]), tile(conv.shape[1]),
            tile(gates.shape[1]),
            _const_spec((None,) + wa.shape[1:], (l, 0, 0)),
            _const_spec((None,) + wp.shape[1:], (l, 0, 0)),
            _const_spec((None,) + wc.shape[1:], (l, 0, 0)),
            _const_spec((None,) + wo.shape[1:], (l, 0, 0)),
            _const_spec((None, 1, d), (l, 0, 0)),
        ],
        out_specs=tile(d),
        out_shape=jax.ShapeDtypeStruct((m, d), F32),
        compiler_params=_params("arbitrary"),
        name="merge",
    )(x2, attn, pool, conv, gates, wa, wp, wc, wo, g)


def _ffn_kernel(x_ref, gpre_ref, win_ref, wout_ref, gpost_ref, o_ref, *, d_ff, chunk):
    x = x_ref[...]
    h = _rms(x, gpre_ref[...]).astype(BF16)
    f = None
    for c in range(0, d_ff, chunk):
        gt = _dot(h, win_ref[:, c:c + chunk])
        up = _dot(h, win_ref[:, d_ff + c:d_ff + c + chunk])
        act = (gt * jax.nn.sigmoid(gt) * up).astype(BF16)
        part = _dot(act, wout_ref[c:c + chunk, :])
        f = part if f is None else f + part
    o_ref[...] = x + _rms(f, gpost_ref[...])


def _ffn(x2, gpre, win, wout, gpost, l):
    m, d = x2.shape
    d_ff = wout.shape[1]
    kern = functools.partial(_ffn_kernel, d_ff=d_ff, chunk=d_ff // 2)
    return pl.pallas_call(
        kern,
        grid=(m // TOKEN_TILE,),
        in_specs=[
            pl.BlockSpec((TOKEN_TILE, d), lambda i: (i, 0)),
            _const_spec((None, 1, d), (l, 0, 0)),
            _const_spec((None, d, 2 * d_ff), (l, 0, 0)),
            _const_spec((None, d_ff, d), (l, 0, 0)),
            _const_spec((None, 1, d), (l, 0, 0)),
        ],
        out_specs=pl.BlockSpec((TOKEN_TILE, d), lambda i: (i, 0)),
        out_shape=jax.ShapeDtypeStruct((m, d), F32),
        compiler_params=_params("arbitrary"),
        name="ffn",
    )(x2, gpre, win, wout, gpost)


def _block_diag(w):
    depth, groups, gd, _ = w.shape
    eye = jnp.eye(groups, dtype=w.dtype)
    return jnp.einsum('lgcd,gh->lgchd', w, eye).reshape(depth, groups * gd, groups * gd)


def kernel(x, pre_mix_g, w_in, b_gate, w_attn_o, rel_bias, w_pool_g, pool_scale, w_pool_o, w_dw,
           b_dw, conv_ln_g, conv_ln_b, w_conv_o, w_out, post_mix_g, pre_ffn_g, w_ffn_in,
           w_ffn_out, post_ffn_g):
    b, s, d = x.shape
    depth = w_in.shape[0]
    attn_w = w_attn_o.shape[1]
    pool_w = w_pool_o.shape[1]
    conv_w = w_conv_o.shape[1]
    u_w = pool_w + 2 * conv_w

    row = lambda a: a.reshape(depth, 1, a.shape[-1])
    w_in_b = w_in.astype(BF16)
    wa_b, wp_b, wc_b, wo_b = (w.astype(BF16) for w in (w_attn_o, w_pool_o, w_conv_o, w_out))
    wfi_b, wfo_b = w_ffn_in.astype(BF16), w_ffn_out.astype(BF16)
    wg_bd = _block_diag(w_pool_g).astype(BF16)
    bias0_t, bias1_t, cfar = _bias_tables(rel_bias)

    x2 = x.reshape(b * s, d)
    for l in range(depth):
        qkv, u, gates = _inproj(x2, row(pre_mix_g), w_in_b, row(b_gate), l, attn_w, u_w)
        attn = _moba(qkv.reshape(b, s, 3 * attn_w), bias0_t, bias1_t, cfar, attn_w)
        pool, conv = _seq_mixers(u.reshape(b, s, u_w), wg_bd, row(pool_scale), w_dw, row(b_dw),
                                 row(conv_ln_g), row(conv_ln_b), l, pool_w, conv_w)
        x2 = _merge(x2, attn.reshape(b * s, attn_w), pool.reshape(b * s, pool_w),
                    conv.reshape(b * s, conv_w), gates, wa_b, wp_b, wc_b, wo_b, row(post_mix_g), l)
        x2 = _ffn(x2, row(pre_ffn_g), wfi_b, wfo_b, row(post_ffn_g), l)
    return x2.reshape(b, s, d)
```

```python
import functools
import math

import jax
import jax.numpy as jnp
from jax import lax
from jax.experimental import pallas as pl
from jax.experimental.pallas import tpu as pltpu

HEAD_DIM = 64
MOBA_BLOCK = 256
MOBA_TOPK = 3
POOL_WINDOWS = (2, 4, 8, 16)
CONV_KERNEL = 31
REL_BUCKETS = 32
REL_MAX_DIST = 128
EPS = 1e-6
NEG = -1e30

LANES = 128
HEADS_PER_STEP = LANES // HEAD_DIM
TOKEN_TILE = 512
VMEM_LIMIT = 56 * 1024 * 1024
SUBLANES = 8
SEQ_PAD = 32
SEQ_TILE = 256
ONES_ROWS = 16

F32 = jnp.float32
BF16 = jnp.bfloat16


def _params(*sem):
    return pltpu.CompilerParams(dimension_semantics=sem, vmem_limit_bytes=VMEM_LIMIT)


def _const_spec(shape, index):
    return pl.BlockSpec(shape, lambda *_: index, pipeline_mode=pl.Buffered(1))


def _rms(x, g):
    return x * lax.rsqrt(jnp.mean(x * x, axis=-1, keepdims=True) + EPS) * g


def _dot(a, b):
    return jnp.dot(a, b, preferred_element_type=F32)


def _inproj_kernel(x_ref, g_ref, w_ref, qkv_ref, u_ref, *, attn_w, chunk):
    h = _rms(x_ref[...], g_ref[...]).astype(BF16)
    qkv_w = 3 * attn_w
    for c in range(0, qkv_w, chunk):
        r = _dot(h, w_ref[:, c:c + chunk])
        if c < attn_w:
            r = r * (HEAD_DIM ** -0.5)
        qkv_ref[:, c:c + chunk] = r.astype(BF16)
    u_ref[...] = _dot(h, w_ref[:, qkv_w:])


def _inproj(x2, g, w, l, attn_w):
    m, d = x2.shape
    u_w = w.shape[2] - 3 * attn_w
    kern = functools.partial(_inproj_kernel, attn_w=attn_w, chunk=512)
    return pl.pallas_call(
        kern,
        grid=(m // TOKEN_TILE,),
        in_specs=[
            pl.BlockSpec((TOKEN_TILE, d), lambda i: (i, 0)),
            _const_spec((None, 1, d), (l, 0, 0)),
            _const_spec((None, d, w.shape[2]), (l, 0, 0)),
        ],
        out_specs=[
            pl.BlockSpec((TOKEN_TILE, 3 * attn_w), lambda i: (i, 0)),
            pl.BlockSpec((TOKEN_TILE, u_w), lambda i: (i, 0)),
        ],
        out_shape=[
            jax.ShapeDtypeStruct((m, 3 * attn_w), BF16),
            jax.ShapeDtypeStruct((m, u_w), F32),
        ],
        compiler_params=_params("arbitrary"),
        name="inproj",
    )(x2, g, w)


def _moba_kernel(q_ref, k_ref, v_ref, b0_ref, b1_ref, cfar_ref, o_ref,
                 kmean_ref, vt_ref, s_ref, p_ref, *, n_blocks):
    blk = MOBA_BLOCK
    for j in range(n_blocks):
        kb = k_ref[0, j * blk:(j + 1) * blk, :].astype(F32)
        kmean_ref[j:j + 1, :] = jnp.mean(kb, axis=0, keepdims=True)
        vt = v_ref[0, j * blk:(j + 1) * blk, :].astype(F32).T
        for h in range(HEADS_PER_STEP):
            vt_ref[h, 0:HEAD_DIM, j * blk:(j + 1) * blk] = (
                vt[h * HEAD_DIM:(h + 1) * HEAD_DIM].astype(BF16))
    for h in range(HEADS_PER_STEP):
        vt_ref[h, HEAD_DIM:HEAD_DIM + ONES_ROWS, :] = jnp.ones((ONES_ROWS, vt_ref.shape[2]), BF16)

    dim_row = lax.broadcasted_iota(jnp.int32, (LANES, blk), 0)
    blk_row = lax.broadcasted_iota(jnp.int32, (n_blocks, blk), 0)
    kmean = kmean_ref[...]

    for qi in range(n_blocks):
        qt = q_ref[0, qi * blk:(qi + 1) * blk, :].astype(F32).T
        outs = []
        for h in range(HEADS_PER_STEP):
            buf = (qi % 2) * HEADS_PER_STEP + h
            in_head = (dim_row >= h * HEAD_DIM) & (dim_row < (h + 1) * HEAD_DIM)
            qt_h = jnp.where(in_head, qt, 0.0)
            qt_hb = qt_h.astype(BF16)

            gate = jnp.dot(kmean, qt_h, preferred_element_type=F32,
                           precision=lax.Precision.HIGHEST)
            rank = jnp.zeros((n_blocks, blk), jnp.int32)
            for j in range(qi):
                other = gate[j:j + 1, :]
                beats = (other > gate) | ((other == gate) & (j < blk_row))
                rank = rank + jnp.where(beats, 1, 0)
            sel = (blk_row < qi) & (rank < MOBA_TOPK)

            cfar = cfar_ref[h, 0:1, 0:1]
            col_max, col_bias = [], []
            n_far = max(qi - 1, 0)
            if n_far:
                r = _dot(k_ref[0, 0:n_far * blk, :], qt_hb)
                s_ref[buf, 0:n_far * blk, :] = r
                for j in range(n_far):
                    col_max.append(jnp.max(r[j * blk:(j + 1) * blk], axis=0, keepdims=True) + cfar)
                    col_bias.append(cfar)
            if qi >= 1:
                lo = (qi - 1) * blk
                r = _dot(k_ref[0, lo:lo + blk, :], qt_hb) + b1_ref[h]
                s_ref[buf, lo:lo + blk, :] = r
                col_max.append(jnp.max(r, axis=0, keepdims=True))
                col_bias.append(None)
            lo = qi * blk
            r = _dot(k_ref[0, lo:lo + blk, :], qt_hb) + b0_ref[h]
            s_ref[buf, lo:lo + blk, :] = r
            m = jnp.max(r, axis=0, keepdims=True)
            chosen = [sel[j:j + 1, :] for j in range(qi)]
            for j in range(qi):
                m = jnp.maximum(m, jnp.where(chosen[j], col_max[j], NEG))

            for j in range(qi + 1):
                if j == qi:
                    off = m
                else:
                    base = m if col_bias[j] is None else m - col_bias[j]
                    off = jnp.where(chosen[j], base, -NEG)
                p = jnp.exp(s_ref[buf, j * blk:(j + 1) * blk, :] - off)
                p_ref[buf, j * blk:(j + 1) * blk, :] = p.astype(BF16)
            n_keys = (qi + 1) * blk
            acc = _dot(vt_ref[h, :, 0:n_keys], p_ref[buf, 0:n_keys, :])
            outs.append(acc[0:HEAD_DIM] / acc[HEAD_DIM:HEAD_DIM + 1])

        o_ref[0, qi * blk:(qi + 1) * blk, :] = jnp.concatenate(outs, axis=0).T.astype(o_ref.dtype)


def _moba(qkv3, bias0_t, bias1_t, cfar, attn_w):
    b, s, _ = qkv3.shape
    n_heads = attn_w // HEAD_DIM
    n_blocks = s // MOBA_BLOCK
    hp = n_heads // HEADS_PER_STEP
    cols = attn_w // LANES
    n_buf = 2 * HEADS_PER_STEP
    kern = functools.partial(_moba_kernel, n_blocks=n_blocks)
    bias_spec = pl.BlockSpec((HEADS_PER_STEP, MOBA_BLOCK, MOBA_BLOCK), lambda p, i: (p, 0, 0))
    return pl.pallas_call(
        kern,
        grid=(hp, b),
        in_specs=[
            pl.BlockSpec((1, s, LANES), lambda p, i: (i, 0, p)),
            pl.BlockSpec((1, s, LANES), lambda p, i: (i, 0, cols + p)),
            pl.BlockSpec((1, s, LANES), lambda p, i: (i, 0, 2 * cols + p)),
            bias_spec,
            bias_spec,
            pl.BlockSpec((HEADS_PER_STEP, 8, LANES), lambda p, i: (p, 0, 0)),
        ],
        out_specs=pl.BlockSpec((1, s, LANES), lambda p, i: (i, 0, p)),
        out_shape=jax.ShapeDtypeStruct((b, s, attn_w), BF16),
        scratch_shapes=[
            pltpu.VMEM((n_blocks, LANES), F32),
            pltpu.VMEM((HEADS_PER_STEP, HEAD_DIM + ONES_ROWS, s), BF16),
            pltpu.VMEM((n_buf, s, MOBA_BLOCK), F32),
            pltpu.VMEM((n_buf, s, MOBA_BLOCK), BF16),
        ],
        compiler_params=_params("arbitrary", "arbitrary"),
        name="moba",
    )(qkv3, qkv3, qkv3, bias0_t, bias1_t, cfar)


def _t5_bucket(rel):
    n = jnp.maximum(rel, 0)
    max_exact = REL_BUCKETS // 2
    nf = jnp.maximum(n, 1).astype(F32)
    large = max_exact + (jnp.log(nf / max_exact) / math.log(REL_MAX_DIST / max_exact)
                         * (REL_BUCKETS - max_exact)).astype(jnp.int32)
    large = jnp.minimum(large, REL_BUCKETS - 1)
    return jnp.where(n < max_exact, n, large)


def _bias_tables(rel_bias):
    table_t = rel_bias.T.astype(F32)
    offs = jnp.arange(MOBA_BLOCK)
    rel_own = offs[None, :] - offs[:, None]
    buckets = jnp.arange(REL_BUCKETS)[:, None, None]

    def lookup(rel):
        onehot = (_t5_bucket(rel)[None] == buckets).astype(F32)
        return jnp.einsum('hb,bkq->hkq', table_t, onehot, precision=lax.Precision.HIGHEST)

    bias0 = jnp.where(rel_own >= 0, lookup(rel_own), NEG)
    bias1 = lookup(rel_own + MOBA_BLOCK)
    far = _t5_bucket(jnp.int32(2 * MOBA_BLOCK - (MOBA_BLOCK - 1)))
    cfar = jnp.broadcast_to(table_t[:, far][:, None, None], (table_t.shape[0], 8, LANES))
    return bias0, bias1, cfar


def _seq_kernel(u_ref, wg_ref, ps_ref, wdw_ref, bdw_ref, lng_ref, lnb_ref,
                pool_ref, conv_ref, rot_ref, *, pool_w, conv_w):
    s = u_ref.shape[1]
    group_w = pool_w // len(POOL_WINDOWS)

    def delayed(delay, t0, lanes):
        a8, r = delay // SUBLANES * SUBLANES, delay % SUBLANES
        return rot_ref[r, pl.ds(SEQ_PAD + t0 - a8, SEQ_TILE), lanes]

    def fill_delayed_copies(t0, lanes, n_copies):
        for r in range(1, n_copies):
            rot_ref[r, pl.ds(SEQ_PAD + t0, SEQ_TILE), lanes] = (
                rot_ref[0, pl.ds(SEQ_PAD + t0 - r, SEQ_TILE), lanes])

    for r in range(SUBLANES):
        rot_ref[r, 0:SEQ_PAD, :] = jnp.zeros((SEQ_PAD, rot_ref.shape[2]), F32)

    lane_tiles = []
    for lo in range(0, pool_w, LANES):
        windows = [w for gi, w in enumerate(POOL_WINDOWS) if lo <= gi * group_w < lo + LANES]
        lane_group = (lax.broadcasted_iota(jnp.int32, (SEQ_TILE, LANES), 1) + lo) // group_w
        window = jnp.zeros((SEQ_TILE, LANES), F32)
        for gi, w in enumerate(POOL_WINDOWS):
            window = jnp.where(lane_group == gi, float(w), window)
        lane_tiles.append((slice(lo, lo + LANES), windows, window))
    row = lax.broadcasted_iota(jnp.int32, (SEQ_TILE, LANES), 0)

    for t0 in range(0, s, SEQ_TILE):
        rot_ref[0, pl.ds(SEQ_PAD + t0, SEQ_TILE), 0:pool_w] = u_ref[0, pl.ds(t0, SEQ_TILE), 0:pool_w]
        for lanes, windows, _ in lane_tiles:
            fill_delayed_copies(t0, lanes, min(SUBLANES, max(windows)))
    for t0 in range(0, s, SEQ_TILE):
        parts = []
        for lanes, windows, window in lane_tiles:
            tok = delayed(0, t0, lanes)
            run = tok
            wsum = jnp.zeros_like(tok)
            for k in range(1, max(windows)):
                run = run + delayed(k, t0, lanes)
                if k + 1 in windows:
                    wsum = jnp.where(window == float(k + 1), run, wsum)
            count = jnp.minimum((row + t0 + 1).astype(F32), window)
            parts.append(wsum / count - tok)
        d = jnp.concatenate(parts, axis=1)
        y = _dot(d.astype(BF16), wg_ref[...]) * ps_ref[...]
        pool_ref[0, pl.ds(t0, SEQ_TILE), :] = y.astype(pool_ref.dtype)

    conv_lanes = slice(0, conv_w)
    for t0 in range(0, s, SEQ_TILE):
        rows = pl.ds(t0, SEQ_TILE)
        a = u_ref[0, rows, pool_w:pool_w + conv_w]
        gt = u_ref[0, rows, pool_w + conv_w:pool_w + 2 * conv_w]
        rot_ref[0, pl.ds(SEQ_PAD + t0, SEQ_TILE), conv_lanes] = a * jax.nn.sigmoid(gt)
        fill_delayed_copies(t0, conv_lanes, SUBLANES)
    for t0 in range(0, s, SEQ_TILE):
        acc = jnp.zeros((SEQ_TILE, conv_w), F32) + bdw_ref[...]
        for k in range(CONV_KERNEL):
            acc = acc + delayed(CONV_KERNEL - 1 - k, t0, conv_lanes) * wdw_ref[k:k + 1, :]
        mu = jnp.mean(acc, axis=-1, keepdims=True)
        cen = acc - mu
        var = jnp.mean(cen * cen, axis=-1, keepdims=True)
        yn = cen * lax.rsqrt(var + EPS) * lng_ref[...] + lnb_ref[...]
        conv_ref[0, pl.ds(t0, SEQ_TILE), :] = (yn * jax.nn.sigmoid(yn)).astype(conv_ref.dtype)


def _seq_mixers(u3, wg_bd, pool_scale, w_dw, b_dw, ln_g, ln_b, l, pool_w, conv_w):
    b, s, u_w = u3.shape
    kern = functools.partial(_seq_kernel, pool_w=pool_w, conv_w=conv_w)
    return pl.pallas_call(
        kern,
        grid=(b,),
        in_specs=[
            pl.BlockSpec((1, s, u_w), lambda i: (i, 0, 0)),
            _const_spec((None, pool_w, pool_w), (l, 0, 0)),
            _const_spec((None, 1, pool_w), (l, 0, 0)),
            _const_spec((None, CONV_KERNEL, conv_w), (l, 0, 0)),
            _const_spec((None, 1, conv_w), (l, 0, 0)),
            _const_spec((None, 1, conv_w), (l, 0, 0)),
            _const_spec((None, 1, conv_w), (l, 0, 0)),
        ],
        out_specs=[
            pl.BlockSpec((1, s, pool_w), lambda i: (i, 0, 0)),
            pl.BlockSpec((1, s, conv_w), lambda i: (i, 0, 0)),
        ],
        out_shape=[
            jax.ShapeDtypeStruct((b, s, pool_w), BF16),
            jax.ShapeDtypeStruct((b, s, conv_w), BF16),
        ],
        scratch_shapes=[pltpu.VMEM((SUBLANES, SEQ_PAD + s, max(pool_w, conv_w)), F32)],
        compiler_params=_params("arbitrary"),
        name="seq_mixers",
    )(u3, wg_bd, pool_scale, w_dw, b_dw, ln_g, ln_b)


def _merge_kernel(x_ref, attn_ref, pool_ref, conv_ref, gpre_ref, wg_ref, bg_ref, wa_ref, wp_ref,
                  wc_ref, wo_ref, gpost_ref, o_ref):
    d = x_ref.shape[1]
    x = x_ref[...]
    h = _rms(x, gpre_ref[...]).astype(BF16)
    merged = None
    branches = ((attn_ref, wa_ref), (pool_ref, wp_ref), (conv_ref, wc_ref))
    for i, (act_ref, w_ref) in enumerate(branches):
        cols = slice(i * d, (i + 1) * d)
        gate = jax.nn.sigmoid(_dot(h, wg_ref[:, cols]) + bg_ref[:, cols])
        part = gate * _dot(act_ref[...], w_ref[...])
        merged = part if merged is None else merged + part
    y = _dot(merged.astype(BF16), wo_ref[...])
    o_ref[...] = x + _rms(y, gpost_ref[...])


def _merge(x2, attn, pool, conv, gpre, wg, bg, wa, wp, wc, wo, gpost, l):
    m, d = x2.shape
    tile = lambda w: pl.BlockSpec((TOKEN_TILE, w), lambda i: (i, 0))
    layer = lambda a: _const_spec((None,) + a.shape[1:], (l, 0, 0))
    return pl.pallas_call(
        _merge_kernel,
        grid=(m // TOKEN_TILE,),
        in_specs=[
            tile(d), tile(attn.shape[1]), tile(pool.shape[1]), tile(conv.shape[1]),
            layer(gpre), layer(wg), layer(bg), layer(wa), layer(wp), layer(wc), layer(wo),
            layer(gpost),
        ],
        out_specs=tile(d),
        out_shape=jax.ShapeDtypeStruct((m, d), F32),
        compiler_params=_params("arbitrary"),
        name="merge",
    )(x2, attn, pool, conv, gpre, wg, bg, wa, wp, wc, wo, gpost)


def _ffn_kernel(x_ref, gpre_ref, win_ref, wout_ref, gpost_ref, o_ref, *, d_ff, chunk):
    x = x_ref[...]
    h = _rms(x, gpre_ref[...]).astype(BF16)
    f = None
    for c in range(0, d_ff, chunk):
        gt = _dot(h, win_ref[:, c:c + chunk])
        up = _dot(h, win_ref[:, d_ff + c:d_ff + c + chunk])
        act = (gt * jax.nn.sigmoid(gt) * up).astype(BF16)
        part = _dot(act, wout_ref[c:c + chunk, :])
        f = part if f is None else f + part
    o_ref[...] = x + _rms(f, gpost_ref[...])


def _ffn(x2, gpre, win, wout, gpost, l):
    m, d = x2.shape
    d_ff = wout.shape[1]
    kern = functools.partial(_ffn_kernel, d_ff=d_ff, chunk=d_ff // 2)
    return pl.pallas_call(
        kern,
        grid=(m // TOKEN_TILE,),
        in_specs=[
            pl.BlockSpec((TOKEN_TILE, d), lambda i: (i, 0)),
            _const_spec((None, 1, d), (l, 0, 0)),
            _const_spec((None, d, 2 * d_ff), (l, 0, 0)),
            _const_spec((None, d_ff, d), (l, 0, 0)),
            _const_spec((None, 1, d), (l, 0, 0)),
        ],
        out_specs=pl.BlockSpec((TOKEN_TILE, d), lambda i: (i, 0)),
        out_shape=jax.ShapeDtypeStruct((m, d), F32),
        compiler_params=_params("arbitrary"),
        name="ffn",
    )(x2, gpre, win, wout, gpost)


def _block_diag(w):
    depth, groups, gd, _ = w.shape
    eye = jnp.eye(groups, dtype=w.dtype)
    return jnp.einsum('lgcd,gh->lgchd', w, eye).reshape(depth, groups * gd, groups * gd)


def kernel(x, pre_mix_g, w_in, b_gate, w_attn_o, rel_bias, w_pool_g, pool_scale, w_pool_o, w_dw,
           b_dw, conv_ln_g, conv_ln_b, w_conv_o, w_out, post_mix_g, pre_ffn_g, w_ffn_in,
           w_ffn_out, post_ffn_g):
    b, s, d = x.shape
    depth = w_in.shape[0]
    attn_w = w_attn_o.shape[1]
    pool_w = w_pool_o.shape[1]
    conv_w = w_conv_o.shape[1]
    u_w = pool_w + 2 * conv_w

    row = lambda a: a.reshape(depth, 1, a.shape[-1])
    act_w = 3 * attn_w + u_w
    w_act_b = w_in[:, :, :act_w].astype(BF16)
    w_gate_b = w_in[:, :, act_w:].astype(BF16)
    wa_b, wp_b, wc_b, wo_b = (w.astype(BF16) for w in (w_attn_o, w_pool_o, w_conv_o, w_out))
    wfi_b, wfo_b = w_ffn_in.astype(BF16), w_ffn_out.astype(BF16)
    wg_bd = _block_diag(w_pool_g).astype(BF16)
    bias0_t, bias1_t, cfar = _bias_tables(rel_bias)

    x2 = x.reshape(b * s, d)
    for l in range(depth):
        qkv, u = _inproj(x2, row(pre_mix_g), w_act_b, l, attn_w)
        attn = _moba(qkv.reshape(b, s, 3 * attn_w), bias0_t, bias1_t, cfar, attn_w)
        pool, conv = _seq_mixers(u.reshape(b, s, u_w), wg_bd, row(pool_scale), w_dw, row(b_dw),
                                 row(conv_ln_g), row(conv_ln_b), l, pool_w, conv_w)
        x2 = _merge(x2, attn.reshape(b * s, attn_w), pool.reshape(b * s, pool_w),
                    conv.reshape(b * s, conv_w), row(pre_mix_g), w_gate_b, row(b_gate),
                    wa_b, wp_b, wc_b, wo_b, row(post_mix_g), l)
        x2 = _ffn(x2, row(pre_ffn_g), wfi_b, wfo_b, row(post_ffn_g), l)
    return x2.reshape(b, s, d)
```

```python
import functools
import math

import jax
import jax.numpy as jnp
from jax import lax
from jax.experimental import pallas as pl
from jax.experimental.pallas import tpu as pltpu

HEAD_DIM = 64
MOBA_BLOCK = 256
MOBA_TOPK = 3
POOL_WINDOWS = (2, 4, 8, 16)
CONV_KERNEL = 31
REL_BUCKETS = 32
REL_MAX_DIST = 128
EPS = 1e-6
NEG = -1e30

LANES = 128
HEADS_PER_STEP = LANES // HEAD_DIM
TOKEN_TILE = 512
VMEM_LIMIT = 56 * 1024 * 1024
SUBLANES = 8
SEQ_PAD = 32
SEQ_TILE = 256
ONES_ROWS = 16

F32 = jnp.float32
BF16 = jnp.bfloat16


def _params(*sem):
    return pltpu.CompilerParams(dimension_semantics=sem, vmem_limit_bytes=VMEM_LIMIT)


def _const_spec(shape, index):
    return pl.BlockSpec(shape, lambda *_: index, pipeline_mode=pl.Buffered(1))


def _rms(x, g):
    return x * lax.rsqrt(jnp.mean(x * x, axis=-1, keepdims=True) + EPS) * g


def _dot(a, b):
    return jnp.dot(a, b, preferred_element_type=F32)


def _inproj_kernel(x_ref, g_ref, w_ref, qkv_ref, u_ref, *, attn_w, chunk):
    h = _rms(x_ref[...], g_ref[...]).astype(BF16)
    qkv_w = 3 * attn_w
    for c in range(0, qkv_w, chunk):
        r = _dot(h, w_ref[:, c:c + chunk])
        if c < attn_w:
            r = r * (HEAD_DIM ** -0.5)
        qkv_ref[:, c:c + chunk] = r.astype(BF16)
    u_ref[...] = _dot(h, w_ref[:, qkv_w:])


def _inproj(x2, g, w, l, attn_w):
    m, d = x2.shape
    u_w = w.shape[2] - 3 * attn_w
    kern = functools.partial(_inproj_kernel, attn_w=attn_w, chunk=512)
    return pl.pallas_call(
        kern,
        grid=(m // TOKEN_TILE,),
        in_specs=[
            pl.BlockSpec((TOKEN_TILE, d), lambda i: (i, 0)),
            _const_spec((None, 1, d), (l, 0, 0)),
            _const_spec((None, d, w.shape[2]), (l, 0, 0)),
        ],
        out_specs=[
            pl.BlockSpec((TOKEN_TILE, 3 * attn_w), lambda i: (i, 0)),
            pl.BlockSpec((TOKEN_TILE, u_w), lambda i: (i, 0)),
        ],
        out_shape=[
            jax.ShapeDtypeStruct((m, 3 * attn_w), BF16),
            jax.ShapeDtypeStruct((m, u_w), F32),
        ],
        compiler_params=_params("arbitrary"),
        name="inproj",
    )(x2, g, w)


def _moba_kernel(q_ref, k_ref, v_ref, b0_ref, b1_ref, cfar_ref, o_ref,
                 kmean_ref, vt_ref, *bufs, n_blocks):
    blk = MOBA_BLOCK
    for j in range(n_blocks):
        kb = k_ref[0, j * blk:(j + 1) * blk, :].astype(F32)
        kmean_ref[j:j + 1, :] = jnp.mean(kb, axis=0, keepdims=True)
        vt = v_ref[0, j * blk:(j + 1) * blk, :].astype(F32).T
        for h in range(HEADS_PER_STEP):
            vt_ref[h, 0:HEAD_DIM, j * blk:(j + 1) * blk] = (
                vt[h * HEAD_DIM:(h + 1) * HEAD_DIM].astype(BF16))
    for h in range(HEADS_PER_STEP):
        vt_ref[h, HEAD_DIM:HEAD_DIM + ONES_ROWS, :] = jnp.ones((ONES_ROWS, vt_ref.shape[2]), BF16)

    dim_row = lax.broadcasted_iota(jnp.int32, (LANES, blk), 0)
    blk_row = lax.broadcasted_iota(jnp.int32, (n_blocks, blk), 0)
    kmean = kmean_ref[...]

    n_buf = len(bufs) // 2
    s_refs, p_refs = bufs[:n_buf], bufs[n_buf:]
    qts = {}

    def pass1(qi, h, s_ref):
        if qi not in qts:
            qts[qi] = q_ref[0, qi * blk:(qi + 1) * blk, :].astype(F32).T
        in_head = (dim_row >= h * HEAD_DIM) & (dim_row < (h + 1) * HEAD_DIM)
        qt_h = jnp.where(in_head, qts[qi], 0.0)
        qt_hb = qt_h.astype(BF16)

        gate = jnp.dot(kmean, qt_h, preferred_element_type=F32,
                       precision=lax.Precision.HIGHEST)
        rank = jnp.zeros((n_blocks, blk), jnp.int32)
        for j in range(qi):
            other = gate[j:j + 1, :]
            beats = (other > gate) | ((other == gate) & (j < blk_row))
            rank = rank + jnp.where(beats, 1, 0)
        sel = (blk_row < qi) & (rank < MOBA_TOPK)

        cfar = cfar_ref[h, 0:1, 0:1]
        col_max, col_bias = [], []
        n_far = max(qi - 1, 0)
        if n_far:
            r = _dot(k_ref[0, 0:n_far * blk, :], qt_hb)
            s_ref[0:n_far * blk, :] = r
            for j in range(n_far):
                col_max.append(jnp.max(r[j * blk:(j + 1) * blk], axis=0, keepdims=True) + cfar)
                col_bias.append(cfar)
        if qi >= 1:
            lo = (qi - 1) * blk
            r = _dot(k_ref[0, lo:lo + blk, :], qt_hb) + b1_ref[h]
            s_ref[lo:lo + blk, :] = r
            col_max.append(jnp.max(r, axis=0, keepdims=True))
            col_bias.append(None)
        lo = qi * blk
        r = _dot(k_ref[0, lo:lo + blk, :], qt_hb) + b0_ref[h]
        s_ref[lo:lo + blk, :] = r
        m = jnp.max(r, axis=0, keepdims=True)
        chosen = [sel[j:j + 1, :] for j in range(qi)]
        for j in range(qi):
            m = jnp.maximum(m, jnp.where(chosen[j], col_max[j], NEG))
        offs = []
        for j in range(qi):
            base = m if col_bias[j] is None else m - col_bias[j]
            offs.append(jnp.where(chosen[j], base, -NEG))
        return offs + [m]

    def pass2(qi, h, offs, s_ref, p_ref):
        for j in range(qi + 1):
            p = jnp.exp(s_ref[j * blk:(j + 1) * blk, :] - offs[j])
            p_ref[j * blk:(j + 1) * blk, :] = p.astype(BF16)
        n_keys = (qi + 1) * blk
        acc = _dot(vt_ref[h, :, 0:n_keys], p_ref[0:n_keys, :])
        return acc[0:HEAD_DIM] / acc[HEAD_DIM:HEAD_DIM + 1]

    units = [(qi, h) for qi in range(n_blocks) for h in range(HEADS_PER_STEP)]
    offs, outs = {}, {}
    for n in range(len(units) + 1):
        if n < len(units):
            offs[n] = pass1(*units[n], s_refs[n % n_buf])
        if n >= 1:
            qi, h = units[n - 1]
            outs[h] = pass2(qi, h, offs.pop(n - 1), s_refs[(n - 1) % n_buf], p_refs[(n - 1) % n_buf])
            if h == HEADS_PER_STEP - 1:
                both = jnp.concatenate([outs[i] for i in range(HEADS_PER_STEP)], axis=0)
                o_ref[0, qi * blk:(qi + 1) * blk, :] = both.T.astype(o_ref.dtype)


def _moba(qkv3, bias0_t, bias1_t, cfar, attn_w):
    b, s, _ = qkv3.shape
    n_heads = attn_w // HEAD_DIM
    n_blocks = s // MOBA_BLOCK
    hp = n_heads // HEADS_PER_STEP
    cols = attn_w // LANES
    n_buf = 2 * HEADS_PER_STEP
    kern = functools.partial(_moba_kernel, n_blocks=n_blocks)
    bias_spec = pl.BlockSpec((HEADS_PER_STEP, MOBA_BLOCK, MOBA_BLOCK), lambda p, i: (p, 0, 0))
    return pl.pallas_call(
        kern,
        grid=(hp, b),
        in_specs=[
            pl.BlockSpec((1, s, LANES), lambda p, i: (i, 0, p)),
            pl.BlockSpec((1, s, LANES), lambda p, i: (i, 0, cols + p)),
            pl.BlockSpec((1, s, LANES), lambda p, i: (i, 0, 2 * cols + p)),
            bias_spec,
            bias_spec,
            pl.BlockSpec((HEADS_PER_STEP, 8, LANES), lambda p, i: (p, 0, 0)),
        ],
        out_specs=pl.BlockSpec((1, s, LANES), lambda p, i: (i, 0, p)),
        out_shape=jax.ShapeDtypeStruct((b, s, attn_w), BF16),
        scratch_shapes=[
            pltpu.VMEM((n_blocks, LANES), F32),
            pltpu.VMEM((HEADS_PER_STEP, HEAD_DIM + ONES_ROWS, s), BF16),
        ] + [pltpu.VMEM((s, MOBA_BLOCK), F32)] * n_buf + [pltpu.VMEM((s, MOBA_BLOCK), BF16)] * n_buf,
        compiler_params=_params("arbitrary", "arbitrary"),
        name="moba",
    )(qkv3, qkv3, qkv3, bias0_t, bias1_t, cfar)


def _t5_bucket(rel):
    n = jnp.maximum(rel, 0)
    max_exact = REL_BUCKETS // 2
    nf = jnp.maximum(n, 1).astype(F32)
    large = max_exact + (jnp.log(nf / max_exact) / math.log(REL_MAX_DIST / max_exact)
                         * (REL_BUCKETS - max_exact)).astype(jnp.int32)
    large = jnp.minimum(large, REL_BUCKETS - 1)
    return jnp.where(n < max_exact, n, large)


def _bias_tables(rel_bias):
    table_t = rel_bias.T.astype(F32)
    offs = jnp.arange(MOBA_BLOCK)
    rel_own = offs[None, :] - offs[:, None]
    buckets = jnp.arange(REL_BUCKETS)[:, None, None]

    def lookup(rel):
        onehot = (_t5_bucket(rel)[None] == buckets).astype(F32)
        return jnp.einsum('hb,bkq->hkq', table_t, onehot, precision=lax.Precision.HIGHEST)

    bias0 = jnp.where(rel_own >= 0, lookup(rel_own), NEG)
    bias1 = lookup(rel_own + MOBA_BLOCK)
    far = _t5_bucket(jnp.int32(2 * MOBA_BLOCK - (MOBA_BLOCK - 1)))
    cfar = jnp.broadcast_to(table_t[:, far][:, None, None], (table_t.shape[0], 8, LANES))
    return bias0, bias1, cfar


def _seq_kernel(u_ref, wg_ref, ps_ref, wdw_ref, bdw_ref, lng_ref, lnb_ref,
                pool_ref, conv_ref, rot_ref, *, pool_w, conv_w):
    s = u_ref.shape[1]
    group_w = pool_w // len(POOL_WINDOWS)

    def delayed(delay, t0, lanes):
        a8, r = delay // SUBLANES * SUBLANES, delay % SUBLANES
        return rot_ref[r, pl.ds(SEQ_PAD + t0 - a8, SEQ_TILE), lanes]

    def fill_delayed_copies(t0, lanes, n_copies):
        for r in range(1, n_copies):
            rot_ref[r, pl.ds(SEQ_PAD + t0, SEQ_TILE), lanes] = (
                rot_ref[0, pl.ds(SEQ_PAD + t0 - r, SEQ_TILE), lanes])

    for r in range(SUBLANES):
        rot_ref[r, 0:SEQ_PAD, :] = jnp.zeros((SEQ_PAD, rot_ref.shape[2]), F32)

    lane_tiles = []
    for lo in range(0, pool_w, LANES):
        windows = [w for gi, w in enumerate(POOL_WINDOWS) if lo <= gi * group_w < lo + LANES]
        lane_group = (lax.broadcasted_iota(jnp.int32, (SEQ_TILE, LANES), 1) + lo) // group_w
        window = jnp.zeros((SEQ_TILE, LANES), F32)
        for gi, w in enumerate(POOL_WINDOWS):
            window = jnp.where(lane_group == gi, float(w), window)
        lane_tiles.append((slice(lo, lo + LANES), windows, window))
    row = lax.broadcasted_iota(jnp.int32, (SEQ_TILE, LANES), 0)

    for t0 in range(0, s, SEQ_TILE):
        rot_ref[0, pl.ds(SEQ_PAD + t0, SEQ_TILE), 0:pool_w] = u_ref[0, pl.ds(t0, SEQ_TILE), 0:pool_w]
        for lanes, windows, _ in lane_tiles:
            fill_delayed_copies(t0, lanes, min(SUBLANES, max(windows)))
    for t0 in range(0, s, SEQ_TILE):
        parts = []
        for lanes, windows, window in lane_tiles:
            tok = delayed(0, t0, lanes)
            run = tok
            wsum = jnp.zeros_like(tok)
            for k in range(1, max(windows)):
                run = run + delayed(k, t0, lanes)
                if k + 1 in windows:
                    wsum = jnp.where(window == float(k + 1), run, wsum)
            count = jnp.minimum((row + t0 + 1).astype(F32), window)
            parts.append(wsum / count - tok)
        d = jnp.concatenate(parts, axis=1)
        y = _dot(d.astype(BF16), wg_ref[...]) * ps_ref[...]
        pool_ref[0, pl.ds(t0, SEQ_TILE), :] = y.astype(pool_ref.dtype)

    conv_lanes = slice(0, conv_w)
    for t0 in range(0, s, SEQ_TILE):
        rows = pl.ds(t0, SEQ_TILE)
        a = u_ref[0, rows, pool_w:pool_w + conv_w]
        gt = u_ref[0, rows, pool_w + conv_w:pool_w + 2 * conv_w]
        rot_ref[0, pl.ds(SEQ_PAD + t0, SEQ_TILE), conv_lanes] = a * jax.nn.sigmoid(gt)
        fill_delayed_copies(t0, conv_lanes, SUBLANES)
    for t0 in range(0, s, SEQ_TILE):
        acc = jnp.zeros((SEQ_TILE, conv_w), F32) + bdw_ref[...]
        for k in range(CONV_KERNEL):
            acc = acc + delayed(CONV_KERNEL - 1 - k, t0, conv_lanes) * wdw_ref[k:k + 1, :]
        mu = jnp.mean(acc, axis=-1, keepdims=True)
        cen = acc - mu
        var = jnp.mean(cen * cen, axis=-1, keepdims=True)
        yn = cen * lax.rsqrt(var + EPS) * lng_ref[...] + lnb_ref[...]
        conv_ref[0, pl.ds(t0, SEQ_TILE), :] = (yn * jax.nn.sigmoid(yn)).astype(conv_ref.dtype)


def _seq_mixers(u3, wg_bd, pool_scale, w_dw, b_dw, ln_g, ln_b, l, pool_w, conv_w):
    b, s, u_w = u3.shape
    kern = functools.partial(_seq_kernel, pool_w=pool_w, conv_w=conv_w)
    return pl.pallas_call(
        kern,
        grid=(b,),
        in_specs=[
            pl.BlockSpec((1, s, u_w), lambda i: (i, 0, 0)),
            _const_spec((None, pool_w, pool_w), (l, 0, 0)),
            _const_spec((None, 1, pool_w), (l, 0, 0)),
            _const_spec((None, CONV_KERNEL, conv_w), (l, 0, 0)),
            _const_spec((None, 1, conv_w), (l, 0, 0)),
            _const_spec((None, 1, conv_w), (l, 0, 0)),
            _const_spec((None, 1, conv_w), (l, 0, 0)),
        ],
        out_specs=[
            pl.BlockSpec((1, s, pool_w), lambda i: (i, 0, 0)),
            pl.BlockSpec((1, s, conv_w), lambda i: (i, 0, 0)),
        ],
        out_shape=[
            jax.ShapeDtypeStruct((b, s, pool_w), BF16),
            jax.ShapeDtypeStruct((b, s, conv_w), BF16),
        ],
        scratch_shapes=[pltpu.VMEM((SUBLANES, SEQ_PAD + s, max(pool_w, conv_w)), F32)],
        compiler_params=_params("arbitrary"),
        name="seq_mixers",
    )(u3, wg_bd, pool_scale, w_dw, b_dw, ln_g, ln_b)


def _merge_kernel(x_ref, attn_ref, pool_ref, conv_ref, gpre_ref, wg_ref, bg_ref, wa_ref, wp_ref,
                  wc_ref, wo_ref, gpost_ref, o_ref):
    d = x_ref.shape[1]
    x = x_ref[...]
    h = _rms(x, gpre_ref[...]).astype(BF16)
    merged = None
    branches = ((attn_ref, wa_ref), (pool_ref, wp_ref), (conv_ref, wc_ref))
    for i, (act_ref, w_ref) in enumerate(branches):
        cols = slice(i * d, (i + 1) * d)
        gate = jax.nn.sigmoid(_dot(h, wg_ref[:, cols]) + bg_ref[:, cols])
        part = gate * _dot(act_ref[...], w_ref[...])
        merged = part if merged is None else merged + part
    y = _dot(merged.astype(BF16), wo_ref[...])
    o_ref[...] = x + _rms(y, gpost_ref[...])


def _merge(x2, attn, pool, conv, gpre, wg, bg, wa, wp, wc, wo, gpost, l):
    m, d = x2.shape
    tile = lambda w: pl.BlockSpec((TOKEN_TILE, w), lambda i: (i, 0))
    layer = lambda a: _const_spec((None,) + a.shape[1:], (l, 0, 0))
    return pl.pallas_call(
        _merge_kernel,
        grid=(m // TOKEN_TILE,),
        in_specs=[
            tile(d), tile(attn.shape[1]), tile(pool.shape[1]), tile(conv.shape[1]),
            layer(gpre), layer(wg), layer(bg), layer(wa), layer(wp), layer(wc), layer(wo),
            layer(gpost),
        ],
        out_specs=tile(d),
        out_shape=jax.ShapeDtypeStruct((m, d), F32),
        compiler_params=_params("arbitrary"),
        name="merge",
    )(x2, attn, pool, conv, gpre, wg, bg, wa, wp, wc, wo, gpost)


def _ffn_kernel(x_ref, gpre_ref, win_ref, wout_ref, gpost_ref, o_ref, *, d_ff, chunk):
    x = x_ref[...]
    h = _rms(x, gpre_ref[...]).astype(BF16)
    f = None
    for c in range(0, d_ff, chunk):
        gt = _dot(h, win_ref[:, c:c + chunk])
        up = _dot(h, win_ref[:, d_ff + c:d_ff + c + chunk])
        act = (gt * jax.nn.sigmoid(gt) * up).astype(BF16)
        part = _dot(act, wout_ref[c:c + chunk, :])
        f = part if f is None else f + part
    o_ref[...] = x + _rms(f, gpost_ref[...])


def _ffn(x2, gpre, win, wout, gpost, l):
    m, d = x2.shape
    d_ff = wout.shape[1]
    kern = functools.partial(_ffn_kernel, d_ff=d_ff, chunk=d_ff // 2)
    return pl.pallas_call(
        kern,
        grid=(m // TOKEN_TILE,),
        in_specs=[
            pl.BlockSpec((TOKEN_TILE, d), lambda i: (i, 0)),
            _const_spec((None, 1, d), (l, 0, 0)),
            _const_spec((None, d, 2 * d_ff), (l, 0, 0)),
            _const_spec((None, d_ff, d), (l, 0, 0)),
            _const_spec((None, 1, d), (l, 0, 0)),
        ],
        out_specs=pl.BlockSpec((TOKEN_TILE, d), lambda i: (i, 0)),
        out_shape=jax.ShapeDtypeStruct((m, d), F32),
        compiler_params=_params("arbitrary"),
        name="ffn",
    )(x2, gpre, win, wout, gpost)


def _block_diag(w):
    depth, groups, gd, _ = w.shape
    eye = jnp.eye(groups, dtype=w.dtype)
    return jnp.einsum('lgcd,gh->lgchd', w, eye).reshape(depth, groups * gd, groups * gd)


def kernel(x, pre_mix_g, w_in, b_gate, w_attn_o, rel_bias, w_pool_g, pool_scale, w_pool_o, w_dw,
           b_dw, conv_ln_g, conv_ln_b, w_conv_o, w_out, post_mix_g, pre_ffn_g, w_ffn_in,
           w_ffn_out, post_ffn_g):
    b, s, d = x.shape
    depth = w_in.shape[0]
    attn_w = w_attn_o.shape[1]
    pool_w = w_pool_o.shape[1]
    conv_w = w_conv_o.shape[1]
    u_w = pool_w + 2 * conv_w

    row = lambda a: a.reshape(depth, 1, a.shape[-1])
    act_w = 3 * attn_w + u_w
    w_act_b = w_in[:, :, :act_w].astype(BF16)
    w_gate_b = w_in[:, :, act_w:].astype(BF16)
    wa_b, wp_b, wc_b, wo_b = (w.astype(BF16) for w in (w_attn_o, w_pool_o, w_conv_o, w_out))
    wfi_b, wfo_b = w_ffn_in.astype(BF16), w_ffn_out.astype(BF16)
    wg_bd = _block_diag(w_pool_g).astype(BF16)
    bias0_t, bias1_t, cfar = _bias_tables(rel_bias)

    x2 = x.reshape(b * s, d)
    for l in range(depth):
        qkv, u = _inproj(x2, row(pre_mix_g), w_act_b, l, attn_w)
        attn = _moba(qkv.reshape(b, s, 3 * attn_w), bias0_t, bias1_t, cfar, attn_w)
        pool, conv = _seq_mixers(u.reshape(b, s, u_w), wg_bd, row(pool_scale), w_dw, row(b_dw),
                                 row(conv_ln_g), row(conv_ln_b), l, pool_w, conv_w)
        x2 = _merge(x2, attn.reshape(b * s, attn_w), pool.reshape(b * s, pool_w),
                    conv.reshape(b * s, conv_w), row(pre_mix_g), w_gate_b, row(b_gate),
                    wa_b, wp_b, wc_b, wo_b, row(post_mix_g), l)
        x2 = _ffn(x2, row(pre_ffn_g), wfi_b, wfo_b, row(post_ffn_g), l)
    return x2.reshape(b, s, d)
```

```python
import functools
import math

import jax
import jax.numpy as jnp
from jax import lax
from jax.experimental import pallas as pl
from jax.experimental.pallas import tpu as pltpu

HEAD_DIM = 64
MOBA_BLOCK = 256
MOBA_TOPK = 3
POOL_WINDOWS = (2, 4, 8, 16)
CONV_KERNEL = 31
REL_BUCKETS = 32
REL_MAX_DIST = 128
EPS = 1e-6
NEG = -1e30
LOG2E = math.log2(math.e)

LANES = 128
HEADS_PER_STEP = LANES // HEAD_DIM
TOKEN_TILE = 512
VMEM_LIMIT = 56 * 1024 * 1024
SUBLANES = 8
SEQ_PAD = 32
SEQ_TILE = 256
ONES_ROWS = 16
MOBA_AHEAD = 1
MOBA_BUFFERS = 4

F32 = jnp.float32
BF16 = jnp.bfloat16


def _params(*sem):
    return pltpu.CompilerParams(dimension_semantics=sem, vmem_limit_bytes=VMEM_LIMIT)


def _const_spec(shape, index):
    return pl.BlockSpec(shape, lambda *_: index, pipeline_mode=pl.Buffered(1))


def _rms(x, g):
    return x * lax.rsqrt(jnp.mean(x * x, axis=-1, keepdims=True) + EPS) * g


def _dot(a, b):
    return jnp.dot(a, b, preferred_element_type=F32)


def _inproj_kernel(x_ref, g_ref, w_ref, qkv_ref, u_ref, *, attn_w, chunk):
    h = _rms(x_ref[...], g_ref[...]).astype(BF16)
    qkv_w = 3 * attn_w
    for c in range(0, qkv_w, chunk):
        r = _dot(h, w_ref[:, c:c + chunk])
        if c < attn_w:
            r = r * (HEAD_DIM ** -0.5)
        qkv_ref[:, c:c + chunk] = r.astype(BF16)
    u_ref[...] = _dot(h, w_ref[:, qkv_w:])


def _inproj(x2, g, w, l, attn_w):
    m, d = x2.shape
    u_w = w.shape[2] - 3 * attn_w
    kern = functools.partial(_inproj_kernel, attn_w=attn_w, chunk=512)
    return pl.pallas_call(
        kern,
        grid=(m // TOKEN_TILE,),
        in_specs=[
            pl.BlockSpec((TOKEN_TILE, d), lambda i: (i, 0)),
            _const_spec((None, 1, d), (l, 0, 0)),
            _const_spec((None, d, w.shape[2]), (l, 0, 0)),
        ],
        out_specs=[
            pl.BlockSpec((TOKEN_TILE, 3 * attn_w), lambda i: (i, 0)),
            pl.BlockSpec((TOKEN_TILE, u_w), lambda i: (i, 0)),
        ],
        out_shape=[
            jax.ShapeDtypeStruct((m, 3 * attn_w), BF16),
            jax.ShapeDtypeStruct((m, u_w), F32),
        ],
        compiler_params=_params("arbitrary"),
        name="inproj",
    )(x2, g, w)


def _moba_kernel(q_ref, k_ref, v_ref, b0_ref, b1_ref, cfar_ref, o_ref,
                 kmean_ref, vt_ref, *bufs, n_blocks):
    blk = MOBA_BLOCK
    for j in range(n_blocks):
        kb = k_ref[0, j * blk:(j + 1) * blk, :].astype(F32)
        kmean_ref[j:j + 1, :] = jnp.mean(kb, axis=0, keepdims=True)
        vt = v_ref[0, j * blk:(j + 1) * blk, :].astype(F32).T
        for h in range(HEADS_PER_STEP):
            vt_ref[h, 0:HEAD_DIM, j * blk:(j + 1) * blk] = (
                vt[h * HEAD_DIM:(h + 1) * HEAD_DIM].astype(BF16))
    for h in range(HEADS_PER_STEP):
        vt_ref[h, HEAD_DIM:HEAD_DIM + ONES_ROWS, :] = jnp.ones((ONES_ROWS, vt_ref.shape[2]), BF16)

    dim_row = lax.broadcasted_iota(jnp.int32, (LANES, blk), 0)
    blk_row = lax.broadcasted_iota(jnp.int32, (n_blocks, blk), 0)
    kmean = kmean_ref[...]

    n_buf = len(bufs) // 2
    s_refs, p_refs = bufs[:n_buf], bufs[n_buf:]
    qts = {}

    def pass1(qi, h, s_ref):
        if qi not in qts:
            qts[qi] = q_ref[0, qi * blk:(qi + 1) * blk, :].astype(F32).T
        in_head = (dim_row >= h * HEAD_DIM) & (dim_row < (h + 1) * HEAD_DIM)
        qt_h = jnp.where(in_head, qts[qi], 0.0)
        qt_hb = (qt_h * LOG2E).astype(BF16)

        gate = jnp.dot(kmean, qt_h, preferred_element_type=F32,
                       precision=lax.Precision.HIGHEST)
        rank = jnp.zeros((n_blocks, blk), jnp.int32)
        for j in range(qi):
            other = gate[j:j + 1, :]
            beats = (other > gate) | ((other == gate) & (j < blk_row))
            rank = rank + jnp.where(beats, 1, 0)
        sel = (blk_row < qi) & (rank < MOBA_TOPK)

        cfar = cfar_ref[h, 0:1, 0:1]
        col_max, col_bias = [], []
        n_far = max(qi - 1, 0)
        if n_far:
            r = _dot(k_ref[0, 0:n_far * blk, :], qt_hb)
            s_ref[0:n_far * blk, :] = r
            for j in range(n_far):
                col_max.append(jnp.max(r[j * blk:(j + 1) * blk], axis=0, keepdims=True) + cfar)
                col_bias.append(cfar)
        if qi >= 1:
            lo = (qi - 1) * blk
            r = _dot(k_ref[0, lo:lo + blk, :], qt_hb) + b1_ref[h]
            s_ref[lo:lo + blk, :] = r
            col_max.append(jnp.max(r, axis=0, keepdims=True))
            col_bias.append(None)
        lo = qi * blk
        r = _dot(k_ref[0, lo:lo + blk, :], qt_hb) + b0_ref[h]
        s_ref[lo:lo + blk, :] = r
        m = jnp.max(r, axis=0, keepdims=True)
        chosen = [sel[j:j + 1, :] for j in range(qi)]
        for j in range(qi):
            m = jnp.maximum(m, jnp.where(chosen[j], col_max[j], NEG))
        offs = []
        for j in range(qi):
            base = m if col_bias[j] is None else m - col_bias[j]
            offs.append(jnp.where(chosen[j], base, -NEG))
        return offs + [m]

    def pass2(qi, h, offs, s_ref, p_ref):
        for j in range(qi + 1):
            x = s_ref[j * blk:(j + 1) * blk, :] - offs[j]
            p_ref[j * blk:(j + 1) * blk, :] = jnp.exp2(x.astype(BF16))
        n_keys = (qi + 1) * blk
        acc = _dot(vt_ref[h, :, 0:n_keys], p_ref[0:n_keys, :])
        return acc[0:HEAD_DIM] / acc[HEAD_DIM:HEAD_DIM + 1]

    units = [(qi, h) for qi in range(n_blocks) for h in range(HEADS_PER_STEP)]
    offs, outs = {}, {}
    for n in range(len(units) + MOBA_AHEAD):
        if n < len(units):
            offs[n] = pass1(*units[n], s_refs[n % n_buf])
        if n >= MOBA_AHEAD:
            d = n - MOBA_AHEAD
            qi, h = units[d]
            outs[h] = pass2(qi, h, offs.pop(d), s_refs[d % n_buf], p_refs[d % n_buf])
            if h == HEADS_PER_STEP - 1:
                both = jnp.concatenate([outs[i] for i in range(HEADS_PER_STEP)], axis=0)
                o_ref[0, qi * blk:(qi + 1) * blk, :] = both.T.astype(o_ref.dtype)


def _moba(qkv3, bias0_t, bias1_t, cfar, attn_w):
    b, s, _ = qkv3.shape
    n_heads = attn_w // HEAD_DIM
    n_blocks = s // MOBA_BLOCK
    hp = n_heads // HEADS_PER_STEP
    cols = attn_w // LANES
    n_buf = MOBA_BUFFERS
    kern = functools.partial(_moba_kernel, n_blocks=n_blocks)
    bias_spec = pl.BlockSpec((HEADS_PER_STEP, MOBA_BLOCK, MOBA_BLOCK), lambda p, i: (p, 0, 0))
    return pl.pallas_call(
        kern,
        grid=(hp, b),
        in_specs=[
            pl.BlockSpec((1, s, LANES), lambda p, i: (i, 0, p)),
            pl.BlockSpec((1, s, LANES), lambda p, i: (i, 0, cols + p)),
            pl.BlockSpec((1, s, LANES), lambda p, i: (i, 0, 2 * cols + p)),
            bias_spec,
            bias_spec,
            pl.BlockSpec((HEADS_PER_STEP, 8, LANES), lambda p, i: (p, 0, 0)),
        ],
        out_specs=pl.BlockSpec((1, s, LANES), lambda p, i: (i, 0, p)),
        out_shape=jax.ShapeDtypeStruct((b, s, attn_w), BF16),
        scratch_shapes=[
            pltpu.VMEM((n_blocks, LANES), F32),
            pltpu.VMEM((HEADS_PER_STEP, HEAD_DIM + ONES_ROWS, s), BF16),
        ] + [pltpu.VMEM((s, MOBA_BLOCK), F32)] * n_buf + [pltpu.VMEM((s, MOBA_BLOCK), BF16)] * n_buf,
        compiler_params=_params("arbitrary", "arbitrary"),
        name="moba",
    )(qkv3, qkv3, qkv3, bias0_t, bias1_t, cfar)


def _t5_bucket(rel):
    n = jnp.maximum(rel, 0)
    max_exact = REL_BUCKETS // 2
    nf = jnp.maximum(n, 1).astype(F32)
    large = max_exact + (jnp.log(nf / max_exact) / math.log(REL_MAX_DIST / max_exact)
                         * (REL_BUCKETS - max_exact)).astype(jnp.int32)
    large = jnp.minimum(large, REL_BUCKETS - 1)
    return jnp.where(n < max_exact, n, large)


def _bias_tables(rel_bias):
    table_t = rel_bias.T.astype(F32)
    offs = jnp.arange(MOBA_BLOCK)
    rel_own = offs[None, :] - offs[:, None]
    buckets = jnp.arange(REL_BUCKETS)[:, None, None]

    def lookup(rel):
        onehot = (_t5_bucket(rel)[None] == buckets).astype(F32)
        return jnp.einsum('hb,bkq->hkq', table_t, onehot, precision=lax.Precision.HIGHEST)

    bias0 = jnp.where(rel_own >= 0, lookup(rel_own), NEG)
    bias1 = lookup(rel_own + MOBA_BLOCK)
    far = _t5_bucket(jnp.int32(2 * MOBA_BLOCK - (MOBA_BLOCK - 1)))
    cfar = jnp.broadcast_to(table_t[:, far][:, None, None], (table_t.shape[0], 8, LANES))
    return bias0 * LOG2E, bias1 * LOG2E, cfar * LOG2E


def _seq_kernel(u_ref, wg_ref, ps_ref, wdw_ref, bdw_ref, lng_ref, lnb_ref,
                pool_ref, conv_ref, rot_ref, *, pool_w, conv_w):
    s = u_ref.shape[1]
    group_w = pool_w // len(POOL_WINDOWS)

    def delayed(delay, t0, lanes):
        a8, r = delay // SUBLANES * SUBLANES, delay % SUBLANES
        return rot_ref[r, pl.ds(SEQ_PAD + t0 - a8, SEQ_TILE), lanes]

    def fill_delayed_copies(t0, lanes, n_copies):
        for r in range(1, n_copies):
            rot_ref[r, pl.ds(SEQ_PAD + t0, SEQ_TILE), lanes] = (
                rot_ref[0, pl.ds(SEQ_PAD + t0 - r, SEQ_TILE), lanes])

    for r in range(SUBLANES):
        rot_ref[r, 0:SEQ_PAD, :] = jnp.zeros((SEQ_PAD, rot_ref.shape[2]), F32)

    lane_tiles = []
    for lo in range(0, pool_w, LANES):
        windows = [w for gi, w in enumerate(POOL_WINDOWS) if lo <= gi * group_w < lo + LANES]
        lane_group = (lax.broadcasted_iota(jnp.int32, (SEQ_TILE, LANES), 1) + lo) // group_w
        window = jnp.zeros((SEQ_TILE, LANES), F32)
        for gi, w in enumerate(POOL_WINDOWS):
            window = jnp.where(lane_group == gi, float(w), window)
        lane_tiles.append((slice(lo, lo + LANES), windows, window))
    row = lax.broadcasted_iota(jnp.int32, (SEQ_TILE, LANES), 0)

    for t0 in range(0, s, SEQ_TILE):
        rot_ref[0, pl.ds(SEQ_PAD + t0, SEQ_TILE), 0:pool_w] = u_ref[0, pl.ds(t0, SEQ_TILE), 0:pool_w]
        for lanes, windows, _ in lane_tiles:
            fill_delayed_copies(t0, lanes, min(SUBLANES, max(windows)))
    for t0 in range(0, s, SEQ_TILE):
        parts = []
        for lanes, windows, window in lane_tiles:
            tok = delayed(0, t0, lanes)
            run = tok
            wsum = jnp.zeros_like(tok)
            for k in range(1, max(windows)):
                run = run + delayed(k, t0, lanes)
                if k + 1 in windows:
                    wsum = jnp.where(window == float(k + 1), run, wsum)
            count = jnp.minimum((row + t0 + 1).astype(F32), window)
            parts.append(wsum / count - tok)
        d = jnp.concatenate(parts, axis=1)
        y = _dot(d.astype(BF16), wg_ref[...]) * ps_ref[...]
        pool_ref[0, pl.ds(t0, SEQ_TILE), :] = y.astype(pool_ref.dtype)

    conv_lanes = slice(0, conv_w)
    for t0 in range(0, s, SEQ_TILE):
        rows = pl.ds(t0, SEQ_TILE)
        a = u_ref[0, rows, pool_w:pool_w + conv_w]
        gt = u_ref[0, rows, pool_w + conv_w:pool_w + 2 * conv_w]
        rot_ref[0, pl.ds(SEQ_PAD + t0, SEQ_TILE), conv_lanes] = a * jax.nn.sigmoid(gt)
        fill_delayed_copies(t0, conv_lanes, SUBLANES)
    for t0 in range(0, s, SEQ_TILE):
        acc = jnp.zeros((SEQ_TILE, conv_w), F32) + bdw_ref[...]
        for k in range(CONV_KERNEL):
            acc = acc + delayed(CONV_KERNEL - 1 - k, t0, conv_lanes) * wdw_ref[k:k + 1, :]
        mu = jnp.mean(acc, axis=-1, keepdims=True)
        cen = acc - mu
        var = jnp.mean(cen * cen, axis=-1, keepdims=True)
        yn = cen * lax.rsqrt(var + EPS) * lng_ref[...] + lnb_ref[...]
        conv_ref[0, pl.ds(t0, SEQ_TILE), :] = (yn * jax.nn.sigmoid(yn)).astype(conv_ref.dtype)


def _seq_mixers(u3, wg_bd, pool_scale, w_dw, b_dw, ln_g, ln_b, l, pool_w, conv_w):
    b, s, u_w = u3.shape
    kern = functools.partial(_seq_kernel, pool_w=pool_w, conv_w=conv_w)
    return pl.pallas_call(
        kern,
        grid=(b,),
        in_specs=[
            pl.BlockSpec((1, s, u_w), lambda i: (i, 0, 0)),
            _const_spec((None, pool_w, pool_w), (l, 0, 0)),
            _const_spec((None, 1, pool_w), (l, 0, 0)),
            _const_spec((None, CONV_KERNEL, conv_w), (l, 0, 0)),
            _const_spec((None, 1, conv_w), (l, 0, 0)),
            _const_spec((None, 1, conv_w), (l, 0, 0)),
            _const_spec((None, 1, conv_w), (l, 0, 0)),
        ],
        out_specs=[
            pl.BlockSpec((1, s, pool_w), lambda i: (i, 0, 0)),
            pl.BlockSpec((1, s, conv_w), lambda i: (i, 0, 0)),
        ],
        out_shape=[
            jax.ShapeDtypeStruct((b, s, pool_w), BF16),
            jax.ShapeDtypeStruct((b, s, conv_w), BF16),
        ],
        scratch_shapes=[pltpu.VMEM((SUBLANES, SEQ_PAD + s, max(pool_w, conv_w)), F32)],
        compiler_params=_params("arbitrary"),
        name="seq_mixers",
    )(u3, wg_bd, pool_scale, w_dw, b_dw, ln_g, ln_b)


def _merge_kernel(x_ref, attn_ref, pool_ref, conv_ref, gpre_ref, wg_ref, bg_ref, wa_ref, wp_ref,
                  wc_ref, wo_ref, gpost_ref, o_ref):
    d = x_ref.shape[1]
    x = x_ref[...]
    h = _rms(x, gpre_ref[...]).astype(BF16)
    merged = None
    branches = ((attn_ref, wa_ref), (pool_ref, wp_ref), (conv_ref, wc_ref))
    for i, (act_ref, w_ref) in enumerate(branches):
        cols = slice(i * d, (i + 1) * d)
        gate = jax.nn.sigmoid(_dot(h, wg_ref[:, cols]) + bg_ref[:, cols])
        part = gate * _dot(act_ref[...], w_ref[...])
        merged = part if merged is None else merged + part
    y = _dot(merged.astype(BF16), wo_ref[...])
    o_ref[...] = x + _rms(y, gpost_ref[...])


def _merge(x2, attn, pool, conv, gpre, wg, bg, wa, wp, wc, wo, gpost, l):
    m, d = x2.shape
    tile = lambda w: pl.BlockSpec((TOKEN_TILE, w), lambda i: (i, 0))
    layer = lambda a: _const_spec((None,) + a.shape[1:], (l, 0, 0))
    return pl.pallas_call(
        _merge_kernel,
        grid=(m // TOKEN_TILE,),
        in_specs=[
            tile(d), tile(attn.shape[1]), tile(pool.shape[1]), tile(conv.shape[1]),
            layer(gpre), layer(wg), layer(bg), layer(wa), layer(wp), layer(wc), layer(wo),
            layer(gpost),
        ],
        out_specs=tile(d),
        out_shape=jax.ShapeDtypeStruct((m, d), F32),
        compiler_params=_params("arbitrary"),
        name="merge",
    )(x2, attn, pool, conv, gpre, wg, bg, wa, wp, wc, wo, gpost)


def _ffn_kernel(x_ref, gpre_ref, win_ref, wout_ref, gpost_ref, o_ref, *, d_ff, chunk):
    x = x_ref[...]
    h = _rms(x, gpre_ref[...]).astype(BF16)
    f = None
    for c in range(0, d_ff, chunk):
        gt = _dot(h, win_ref[:, c:c + chunk])
        up = _dot(h, win_ref[:, d_ff + c:d_ff + c + chunk])
        act = (gt * jax.nn.sigmoid(gt) * up).astype(BF16)
        part = _dot(act, wout_ref[c:c + chunk, :])
        f = part if f is None else f + part
    o_ref[...] = x + _rms(f, gpost_ref[...])


def _ffn(x2, gpre, win, wout, gpost, l):
    m, d = x2.shape
    d_ff = wout.shape[1]
    kern = functools.partial(_ffn_kernel, d_ff=d_ff, chunk=d_ff // 2)
    return pl.pallas_call(
        kern,
        grid=(m // TOKEN_TILE,),
        in_specs=[
            pl.BlockSpec((TOKEN_TILE, d), lambda i: (i, 0)),
            _const_spec((None, 1, d), (l, 0, 0)),
            _const_spec((None, d, 2 * d_ff), (l, 0, 0)),
            _const_spec((None, d_ff, d), (l, 0, 0)),
            _const_spec((None, 1, d), (l, 0, 0)),
        ],
        out_specs=pl.BlockSpec((TOKEN_TILE, d), lambda i: (i, 0)),
        out_shape=jax.ShapeDtypeStruct((m, d), F32),
        compiler_params=_params("arbitrary"),
        name="ffn",
    )(x2, gpre, win, wout, gpost)


def _block_diag(w):
    depth, groups, gd, _ = w.shape
    eye = jnp.eye(groups, dtype=w.dtype)
    return jnp.einsum('lgcd,gh->lgchd', w, eye).reshape(depth, groups * gd, groups * gd)


def kernel(x, pre_mix_g, w_in, b_gate, w_attn_o, rel_bias, w_pool_g, pool_scale, w_pool_o, w_dw,
           b_dw, conv_ln_g, conv_ln_b, w_conv_o, w_out, post_mix_g, pre_ffn_g, w_ffn_in,
           w_ffn_out, post_ffn_g):
    b, s, d = x.shape
    depth = w_in.shape[0]
    attn_w = w_attn_o.shape[1]
    pool_w = w_pool_o.shape[1]
    conv_w = w_conv_o.shape[1]
    u_w = pool_w + 2 * conv_w

    row = lambda a: a.reshape(depth, 1, a.shape[-1])
    act_w = 3 * attn_w + u_w
    w_act_b = w_in[:, :, :act_w].astype(BF16)
    w_gate_b = w_in[:, :, act_w:].astype(BF16)
    wa_b, wp_b, wc_b, wo_b = (w.astype(BF16) for w in (w_attn_o, w_pool_o, w_conv_o, w_out))
    wfi_b, wfo_b = w_ffn_in.astype(BF16), w_ffn_out.astype(BF16)
    wg_bd = _block_diag(w_pool_g).astype(BF16)
    bias0_t, bias1_t, cfar = _bias_tables(rel_bias)

    x2 = x.reshape(b * s, d)
    for l in range(depth):
        qkv, u = _inproj(x2, row(pre_mix_g), w_act_b, l, attn_w)
        attn = _moba(qkv.reshape(b, s, 3 * attn_w), bias0_t, bias1_t, cfar, attn_w)
        pool, conv = _seq_mixers(u.reshape(b, s, u_w), wg_bd, row(pool_scale), w_dw, row(b_dw),
                                 row(conv_ln_g), row(conv_ln_b), l, pool_w, conv_w)
        x2 = _merge(x2, attn.reshape(b * s, attn_w), pool.reshape(b * s, pool_w),
                    conv.reshape(b * s, conv_w), row(pre_mix_g), w_gate_b, row(b_gate),
                    wa_b, wp_b, wc_b, wo_b, row(post_mix_g), l)
        x2 = _ffn(x2, row(pre_ffn_g), wfi_b, wfo_b, row(post_ffn_g), l)
    return x2.reshape(b, s, d)
```

```python
import functools
import math

import jax
import jax.numpy as jnp
from jax import lax
from jax.experimental import pallas as pl
from jax.experimental.pallas import tpu as pltpu

HEAD_DIM = 64
MOBA_BLOCK = 256
MOBA_TOPK = 3
POOL_WINDOWS = (2, 4, 8, 16)
CONV_KERNEL = 31
REL_BUCKETS = 32
REL_MAX_DIST = 128
EPS = 1e-6
NEG = -1e30
LOG2E = math.log2(math.e)

LANES = 128
HEADS_PER_STEP = LANES // HEAD_DIM
TOKEN_TILE = 1024
INPROJ_SUBTILES = 2
MERGE_SUBTILES = 4
FFN_SUBTILES = 4
FFN_CHUNK = 1024
VMEM_LIMIT = 56 * 1024 * 1024
SUBLANES = 8
SEQ_PAD = 32
SEQ_TILE = 256
ONES_ROWS = 16
MOBA_AHEAD = 1
MOBA_BUFFERS = 4

F32 = jnp.float32
BF16 = jnp.bfloat16


def _params(*sem):
    return pltpu.CompilerParams(dimension_semantics=sem, vmem_limit_bytes=VMEM_LIMIT)


def _const_spec(shape, index):
    return pl.BlockSpec(shape, lambda *_: index, pipeline_mode=pl.Buffered(1))


def _rms(x, g):
    return x * lax.rsqrt(jnp.mean(x * x, axis=-1, keepdims=True) + EPS) * g


def _dot(a, b):
    return jnp.dot(a, b, preferred_element_type=F32)


def _row_groups(n_rows, n_groups):
    size = n_rows // n_groups
    return [slice(i * size, (i + 1) * size) for i in range(n_groups)]


def _inproj_kernel(x_ref, g_ref, w_ref, qkv_ref, u_ref, *, attn_w, chunk):
    qkv_w = 3 * attn_w
    groups = _row_groups(x_ref.shape[0], INPROJ_SUBTILES)
    hs = [_rms(x_ref[rows, :], g_ref[...]).astype(BF16) for rows in groups]
    for rows, h in zip(groups, hs):
        for c in range(0, qkv_w, chunk):
            r = _dot(h, w_ref[:, c:c + chunk])
            if c < attn_w:
                r = r * (HEAD_DIM ** -0.5)
            qkv_ref[rows, c:c + chunk] = r.astype(BF16)
        u_ref[rows, :] = _dot(h, w_ref[:, qkv_w:])


def _inproj(x2, g, w, l, attn_w):
    m, d = x2.shape
    u_w = w.shape[2] - 3 * attn_w
    kern = functools.partial(_inproj_kernel, attn_w=attn_w, chunk=512)
    return pl.pallas_call(
        kern,
        grid=(m // TOKEN_TILE,),
        in_specs=[
            pl.BlockSpec((TOKEN_TILE, d), lambda i: (i, 0)),
            _const_spec((None, 1, d), (l, 0, 0)),
            _const_spec((None, d, w.shape[2]), (l, 0, 0)),
        ],
        out_specs=[
            pl.BlockSpec((TOKEN_TILE, 3 * attn_w), lambda i: (i, 0)),
            pl.BlockSpec((TOKEN_TILE, u_w), lambda i: (i, 0)),
        ],
        out_shape=[
            jax.ShapeDtypeStruct((m, 3 * attn_w), BF16),
            jax.ShapeDtypeStruct((m, u_w), F32),
        ],
        compiler_params=_params("arbitrary"),
        name="inproj",
    )(x2, g, w)


def _moba_kernel(q_ref, k_ref, v_ref, b0_ref, b1_ref, cfar_ref, o_ref,
                 kmean_ref, vt_ref, *bufs, n_blocks):
    blk = MOBA_BLOCK
    for j in range(n_blocks):
        kb = k_ref[0, j * blk:(j + 1) * blk, :].astype(F32)
        kmean_ref[j:j + 1, :] = jnp.mean(kb, axis=0, keepdims=True)
        vt = v_ref[0, j * blk:(j + 1) * blk, :].astype(F32).T
        for h in range(HEADS_PER_STEP):
            vt_ref[h, 0:HEAD_DIM, j * blk:(j + 1) * blk] = (
                vt[h * HEAD_DIM:(h + 1) * HEAD_DIM].astype(BF16))
    for h in range(HEADS_PER_STEP):
        vt_ref[h, HEAD_DIM:HEAD_DIM + ONES_ROWS, :] = jnp.ones((ONES_ROWS, vt_ref.shape[2]), BF16)

    dim_row = lax.broadcasted_iota(jnp.int32, (LANES, blk), 0)
    blk_row = lax.broadcasted_iota(jnp.int32, (n_blocks, blk), 0)
    kmean = kmean_ref[...]

    n_buf = len(bufs) // 2
    s_refs, p_refs = bufs[:n_buf], bufs[n_buf:]
    qts = {}

    def pass1(qi, h, s_ref):
        if qi not in qts:
            qts[qi] = q_ref[0, qi * blk:(qi + 1) * blk, :].astype(F32).T
        in_head = (dim_row >= h * HEAD_DIM) & (dim_row < (h + 1) * HEAD_DIM)
        qt_h = jnp.where(in_head, qts[qi], 0.0)
        qt_hb = (qt_h * LOG2E).astype(BF16)

        gate = jnp.dot(kmean, qt_h, preferred_element_type=F32,
                       precision=lax.Precision.HIGHEST)
        rank = jnp.zeros((n_blocks, blk), jnp.int32)
        for j in range(qi):
            other = gate[j:j + 1, :]
            beats = (other > gate) | ((other == gate) & (j < blk_row))
            rank = rank + jnp.where(beats, 1, 0)
        sel = (blk_row < qi) & (rank < MOBA_TOPK)

        cfar = cfar_ref[h, 0:1, 0:1]
        col_max, col_bias = [], []
        n_far = max(qi - 1, 0)
        if n_far:
            r = _dot(k_ref[0, 0:n_far * blk, :], qt_hb)
            s_ref[0:n_far * blk, :] = r
            for j in range(n_far):
                col_max.append(jnp.max(r[j * blk:(j + 1) * blk], axis=0, keepdims=True) + cfar)
                col_bias.append(cfar)
        if qi >= 1:
            lo = (qi - 1) * blk
            r = _dot(k_ref[0, lo:lo + blk, :], qt_hb) + b1_ref[h]
            s_ref[lo:lo + blk, :] = r
            col_max.append(jnp.max(r, axis=0, keepdims=True))
            col_bias.append(None)
        lo = qi * blk
        r = _dot(k_ref[0, lo:lo + blk, :], qt_hb) + b0_ref[h]
        s_ref[lo:lo + blk, :] = r
        m = jnp.max(r, axis=0, keepdims=True)
        chosen = [sel[j:j + 1, :] for j in range(qi)]
        for j in range(qi):
            m = jnp.maximum(m, jnp.where(chosen[j], col_max[j], NEG))
        offs = []
        for j in range(qi):
            base = m if col_bias[j] is None else m - col_bias[j]
            offs.append(jnp.where(chosen[j], base, -NEG))
        return offs + [m]

    def pass2(qi, h, offs, s_ref, p_ref):
        for j in range(qi + 1):
            x = s_ref[j * blk:(j + 1) * blk, :] - offs[j]
            p_ref[j * blk:(j + 1) * blk, :] = jnp.exp2(x.astype(BF16))
        n_keys = (qi + 1) * blk
        acc = _dot(vt_ref[h, :, 0:n_keys], p_ref[0:n_keys, :])
        return acc[0:HEAD_DIM] / acc[HEAD_DIM:HEAD_DIM + 1]

    units = [(qi, h) for qi in range(n_blocks) for h in range(HEADS_PER_STEP)]
    offs, outs = {}, {}
    for n in range(len(units) + MOBA_AHEAD):
        if n < len(units):
            offs[n] = pass1(*units[n], s_refs[n % n_buf])
        if n >= MOBA_AHEAD:
            d = n - MOBA_AHEAD
            qi, h = units[d]
            outs[h] = pass2(qi, h, offs.pop(d), s_refs[d % n_buf], p_refs[d % n_buf])
            if h == HEADS_PER_STEP - 1:
                both = jnp.concatenate([outs[i] for i in range(HEADS_PER_STEP)], axis=0)
                o_ref[0, qi * blk:(qi + 1) * blk, :] = both.T.astype(o_ref.dtype)


def _moba(qkv3, bias0_t, bias1_t, cfar, attn_w):
    b, s, _ = qkv3.shape
    n_heads = attn_w // HEAD_DIM
    n_blocks = s // MOBA_BLOCK
    hp = n_heads // HEADS_PER_STEP
    cols = attn_w // LANES
    n_buf = MOBA_BUFFERS
    kern = functools.partial(_moba_kernel, n_blocks=n_blocks)
    bias_spec = pl.BlockSpec((HEADS_PER_STEP, MOBA_BLOCK, MOBA_BLOCK), lambda p, i: (p, 0, 0))
    return pl.pallas_call(
        kern,
        grid=(hp, b),
        in_specs=[
            pl.BlockSpec((1, s, LANES), lambda p, i: (i, 0, p)),
            pl.BlockSpec((1, s, LANES), lambda p, i: (i, 0, cols + p)),
            pl.BlockSpec((1, s, LANES), lambda p, i: (i, 0, 2 * cols + p)),
            bias_spec,
            bias_spec,
            pl.BlockSpec((HEADS_PER_STEP, 8, LANES), lambda p, i: (p, 0, 0)),
        ],
        out_specs=pl.BlockSpec((1, s, LANES), lambda p, i: (i, 0, p)),
        out_shape=jax.ShapeDtypeStruct((b, s, attn_w), BF16),
        scratch_shapes=[
            pltpu.VMEM((n_blocks, LANES), F32),
            pltpu.VMEM((HEADS_PER_STEP, HEAD_DIM + ONES_ROWS, s), BF16),
        ] + [pltpu.VMEM((s, MOBA_BLOCK), F32)] * n_buf + [pltpu.VMEM((s, MOBA_BLOCK), BF16)] * n_buf,
        compiler_params=_params("arbitrary", "arbitrary"),
        name="moba",
    )(qkv3, qkv3, qkv3, bias0_t, bias1_t, cfar)


def _t5_bucket(rel):
    n = jnp.maximum(rel, 0)
    max_exact = REL_BUCKETS // 2
    nf = jnp.maximum(n, 1).astype(F32)
    large = max_exact + (jnp.log(nf / max_exact) / math.log(REL_MAX_DIST / max_exact)
                         * (REL_BUCKETS - max_exact)).astype(jnp.int32)
    large = jnp.minimum(large, REL_BUCKETS - 1)
    return jnp.where(n < max_exact, n, large)


def _bias_tables(rel_bias):
    table_t = rel_bias.T.astype(F32)
    offs = jnp.arange(MOBA_BLOCK)
    rel_own = offs[None, :] - offs[:, None]
    buckets = jnp.arange(REL_BUCKETS)[:, None, None]

    def lookup(rel):
        onehot = (_t5_bucket(rel)[None] == buckets).astype(F32)
        return jnp.einsum('hb,bkq->hkq', table_t, onehot, precision=lax.Precision.HIGHEST)

    bias0 = jnp.where(rel_own >= 0, lookup(rel_own), NEG)
    bias1 = lookup(rel_own + MOBA_BLOCK)
    far = _t5_bucket(jnp.int32(2 * MOBA_BLOCK - (MOBA_BLOCK - 1)))
    cfar = jnp.broadcast_to(table_t[:, far][:, None, None], (table_t.shape[0], 8, LANES))
    return bias0 * LOG2E, bias1 * LOG2E, cfar * LOG2E


def _seq_kernel(u_ref, wg_ref, ps_ref, wdw_ref, bdw_ref, lng_ref, lnb_ref,
                pool_ref, conv_ref, rot_ref, *, pool_w, conv_w):
    s = u_ref.shape[1]
    group_w = pool_w // len(POOL_WINDOWS)

    def delayed(delay, t0, lanes):
        a8, r = delay // SUBLANES * SUBLANES, delay % SUBLANES
        return rot_ref[r, pl.ds(SEQ_PAD + t0 - a8, SEQ_TILE), lanes]

    def fill_delayed_copies(t0, lanes, n_copies):
        for r in range(1, n_copies):
            rot_ref[r, pl.ds(SEQ_PAD + t0, SEQ_TILE), lanes] = (
                rot_ref[0, pl.ds(SEQ_PAD + t0 - r, SEQ_TILE), lanes])

    for r in range(SUBLANES):
        rot_ref[r, 0:SEQ_PAD, :] = jnp.zeros((SEQ_PAD, rot_ref.shape[2]), F32)

    lane_tiles = []
    for lo in range(0, pool_w, LANES):
        windows = [w for gi, w in enumerate(POOL_WINDOWS) if lo <= gi * group_w < lo + LANES]
        lane_group = (lax.broadcasted_iota(jnp.int32, (SEQ_TILE, LANES), 1) + lo) // group_w
        window = jnp.zeros((SEQ_TILE, LANES), F32)
        for gi, w in enumerate(POOL_WINDOWS):
            window = jnp.where(lane_group == gi, float(w), window)
        lane_tiles.append((slice(lo, lo + LANES), windows, window))
    row = lax.broadcasted_iota(jnp.int32, (SEQ_TILE, LANES), 0)

    for t0 in range(0, s, SEQ_TILE):
        rot_ref[0, pl.ds(SEQ_PAD + t0, SEQ_TILE), 0:pool_w] = u_ref[0, pl.ds(t0, SEQ_TILE), 0:pool_w]
        for lanes, windows, _ in lane_tiles:
            fill_delayed_copies(t0, lanes, min(SUBLANES, max(windows)))
    for t0 in range(0, s, SEQ_TILE):
        parts = []
        for lanes, windows, window in lane_tiles:
            tok = delayed(0, t0, lanes)
            run = tok
            wsum = jnp.zeros_like(tok)
            for k in range(1, max(windows)):
                run = run + delayed(k, t0, lanes)
                if k + 1 in windows:
                    wsum = jnp.where(window == float(k + 1), run, wsum)
            count = jnp.minimum((row + t0 + 1).astype(F32), window)
            parts.append(wsum / count - tok)
        d = jnp.concatenate(parts, axis=1)
        y = _dot(d.astype(BF16), wg_ref[...]) * ps_ref[...]
        pool_ref[0, pl.ds(t0, SEQ_TILE), :] = y.astype(pool_ref.dtype)

    conv_lanes = slice(0, conv_w)
    for t0 in range(0, s, SEQ_TILE):
        rows = pl.ds(t0, SEQ_TILE)
        a = u_ref[0, rows, pool_w:pool_w + conv_w]
        gt = u_ref[0, rows, pool_w + conv_w:pool_w + 2 * conv_w]
        rot_ref[0, pl.ds(SEQ_PAD + t0, SEQ_TILE), conv_lanes] = a * jax.nn.sigmoid(gt)
        fill_delayed_copies(t0, conv_lanes, SUBLANES)
    for t0 in range(0, s, SEQ_TILE):
        acc = jnp.zeros((SEQ_TILE, conv_w), F32) + bdw_ref[...]
        for k in range(CONV_KERNEL):
            acc = acc + delayed(CONV_KERNEL - 1 - k, t0, conv_lanes) * wdw_ref[k:k + 1, :]
        mu = jnp.mean(acc, axis=-1, keepdims=True)
        cen = acc - mu
        var = jnp.mean(cen * cen, axis=-1, keepdims=True)
        yn = cen * lax.rsqrt(var + EPS) * lng_ref[...] + lnb_ref[...]
        conv_ref[0, pl.ds(t0, SEQ_TILE), :] = (yn * jax.nn.sigmoid(yn)).astype(conv_ref.dtype)


def _seq_mixers(u3, wg_bd, pool_scale, w_dw, b_dw, ln_g, ln_b, l, pool_w, conv_w):
    b, s, u_w = u3.shape
    kern = functools.partial(_seq_kernel, pool_w=pool_w, conv_w=conv_w)
    return pl.pallas_call(
        kern,
        grid=(b,),
        in_specs=[
            pl.BlockSpec((1, s, u_w), lambda i: (i, 0, 0)),
            _const_spec((None, pool_w, pool_w), (l, 0, 0)),
            _const_spec((None, 1, pool_w), (l, 0, 0)),
            _const_spec((None, CONV_KERNEL, conv_w), (l, 0, 0)),
            _const_spec((None, 1, conv_w), (l, 0, 0)),
            _const_spec((None, 1, conv_w), (l, 0, 0)),
            _const_spec((None, 1, conv_w), (l, 0, 0)),
        ],
        out_specs=[
            pl.BlockSpec((1, s, pool_w), lambda i: (i, 0, 0)),
            pl.BlockSpec((1, s, conv_w), lambda i: (i, 0, 0)),
        ],
        out_shape=[
            jax.ShapeDtypeStruct((b, s, pool_w), BF16),
            jax.ShapeDtypeStruct((b, s, conv_w), BF16),
        ],
        scratch_shapes=[pltpu.VMEM((SUBLANES, SEQ_PAD + s, max(pool_w, conv_w)), F32)],
        compiler_params=_params("arbitrary"),
        name="seq_mixers",
    )(u3, wg_bd, pool_scale, w_dw, b_dw, ln_g, ln_b)


def _merge_kernel(x_ref, attn_ref, pool_ref, conv_ref, gpre_ref, wg_ref, bg_ref, wa_ref, wp_ref,
                  wc_ref, wo_ref, gpost_ref, o_ref):
    d = x_ref.shape[1]
    branches = ((attn_ref, wa_ref), (pool_ref, wp_ref), (conv_ref, wc_ref))
    groups = _row_groups(x_ref.shape[0], MERGE_SUBTILES)
    xs = [x_ref[rows, :] for rows in groups]
    hs = [_rms(x, gpre_ref[...]).astype(BF16) for x in xs]
    ys = []
    for rows, h in zip(groups, hs):
        merged = None
        for i, (act_ref, w_ref) in enumerate(branches):
            cols = slice(i * d, (i + 1) * d)
            gate = jax.nn.sigmoid(_dot(h, wg_ref[:, cols]) + bg_ref[:, cols])
            part = gate * _dot(act_ref[rows, :], w_ref[...])
            merged = part if merged is None else merged + part
        ys.append(_dot(merged.astype(BF16), wo_ref[...]))
    for rows, x, y in zip(groups, xs, ys):
        o_ref[rows, :] = x + _rms(y, gpost_ref[...])


def _merge(x2, attn, pool, conv, gpre, wg, bg, wa, wp, wc, wo, gpost, l):
    m, d = x2.shape
    tile = lambda w: pl.BlockSpec((TOKEN_TILE, w), lambda i: (i, 0))
    layer = lambda a: _const_spec((None,) + a.shape[1:], (l, 0, 0))
    return pl.pallas_call(
        _merge_kernel,
        grid=(m // TOKEN_TILE,),
        in_specs=[
            tile(d), tile(attn.shape[1]), tile(pool.shape[1]), tile(conv.shape[1]),
            layer(gpre), layer(wg), layer(bg), layer(wa), layer(wp), layer(wc), layer(wo),
            layer(gpost),
        ],
        out_specs=tile(d),
        out_shape=jax.ShapeDtypeStruct((m, d), F32),
        compiler_params=_params("arbitrary"),
        name="merge",
    )(x2, attn, pool, conv, gpre, wg, bg, wa, wp, wc, wo, gpost)


def _ffn_kernel(x_ref, gpre_ref, win_ref, wout_ref, gpost_ref, o_ref, *, d_ff):
    groups = _row_groups(x_ref.shape[0], FFN_SUBTILES)
    xs = [x_ref[rows, :] for rows in groups]
    hs = [_rms(x, gpre_ref[...]).astype(BF16) for x in xs]
    fs = []
    for h in hs:
        f = None
        for c in range(0, d_ff, FFN_CHUNK):
            w = min(FFN_CHUNK, d_ff - c)
            gt = _dot(h, win_ref[:, c:c + w])
            up = _dot(h, win_ref[:, d_ff + c:d_ff + c + w])
            act = (gt * jax.nn.sigmoid(gt) * up).astype(BF16)
            part = _dot(act, wout_ref[c:c + w, :])
            f = part if f is None else f + part
        fs.append(f)
    for rows, x, f in zip(groups, xs, fs):
        o_ref[rows, :] = x + _rms(f, gpost_ref[...])


def _ffn(x2, gpre, win, wout, gpost, l):
    m, d = x2.shape
    d_ff = wout.shape[1]
    kern = functools.partial(_ffn_kernel, d_ff=d_ff)
    return pl.pallas_call(
        kern,
        grid=(m // TOKEN_TILE,),
        in_specs=[
            pl.BlockSpec((TOKEN_TILE, d), lambda i: (i, 0)),
            _const_spec((None, 1, d), (l, 0, 0)),
            _const_spec((None, d, 2 * d_ff), (l, 0, 0)),
            _const_spec((None, d_ff, d), (l, 0, 0)),
            _const_spec((None, 1, d), (l, 0, 0)),
        ],
        out_specs=pl.BlockSpec((TOKEN_TILE, d), lambda i: (i, 0)),
        out_shape=jax.ShapeDtypeStruct((m, d), F32),
        compiler_params=_params("arbitrary"),
        name="ffn",
    )(x2, gpre, win, wout, gpost)


def _block_diag(w):
    depth, groups, gd, _ = w.shape
    eye = jnp.eye(groups, dtype=w.dtype)
    return jnp.einsum('lgcd,gh->lgchd', w, eye).reshape(depth, groups * gd, groups * gd)


def kernel(x, pre_mix_g, w_in, b_gate, w_attn_o, rel_bias, w_pool_g, pool_scale, w_pool_o, w_dw,
           b_dw, conv_ln_g, conv_ln_b, w_conv_o, w_out, post_mix_g, pre_ffn_g, w_ffn_in,
           w_ffn_out, post_ffn_g):
    b, s, d = x.shape
    depth = w_in.shape[0]
    attn_w = w_attn_o.shape[1]
    pool_w = w_pool_o.shape[1]
    conv_w = w_conv_o.shape[1]
    u_w = pool_w + 2 * conv_w

    row = lambda a: a.reshape(depth, 1, a.shape[-1])
    act_w = 3 * attn_w + u_w
    w_act_b = w_in[:, :, :act_w].astype(BF16)
    w_gate_b = w_in[:, :, act_w:].astype(BF16)
    wa_b, wp_b, wc_b, wo_b = (w.astype(BF16) for w in (w_attn_o, w_pool_o, w_conv_o, w_out))
    wfi_b, wfo_b = w_ffn_in.astype(BF16), w_ffn_out.astype(BF16)
    wg_bd = _block_diag(w_pool_g).astype(BF16)
    bias0_t, bias1_t, cfar = _bias_tables(rel_bias)

    x2 = x.reshape(b * s, d)
    for l in range(depth):
        qkv, u = _inproj(x2, row(pre_mix_g), w_act_b, l, attn_w)
        attn = _moba(qkv.reshape(b, s, 3 * attn_w), bias0_t, bias1_t, cfar, attn_w)
        pool, conv = _seq_mixers(u.reshape(b, s, u_w), wg_bd, row(pool_scale), w_dw, row(b_dw),
                                 row(conv_ln_g), row(conv_ln_b), l, pool_w, conv_w)
        x2 = _merge(x2, attn.reshape(b * s, attn_w), pool.reshape(b * s, pool_w),
                    conv.reshape(b * s, conv_w), row(pre_mix_g), w_gate_b, row(b_gate),
                    wa_b, wp_b, wc_b, wo_b, row(post_mix_g), l)
        x2 = _ffn(x2, row(pre_ffn_g), wfi_b, wfo_b, row(post_ffn_g), l)
    return x2.reshape(b, s, d)
```

```python
import functools
import math

import jax
import jax.numpy as jnp
from jax import lax
from jax.experimental import pallas as pl
from jax.experimental.pallas import tpu as pltpu

HEAD_DIM = 64
MOBA_BLOCK = 256
MOBA_TOPK = 3
POOL_WINDOWS = (2, 4, 8, 16)
CONV_KERNEL = 31
REL_BUCKETS = 32
REL_MAX_DIST = 128
EPS = 1e-6
NEG = -1e30
LOG2E = math.log2(math.e)

LANES = 128
HEADS_PER_STEP = LANES // HEAD_DIM
TOKEN_TILE = 1024
INPROJ_SUBTILES = 2
MERGE_SUBTILES = 4
FFN_SUBTILES = 4
FFN_CHUNK = 1024
VMEM_LIMIT = 56 * 1024 * 1024
SUBLANES = 8
SEQ_PAD = 32
SEQ_TILE = 256
ONES_ROWS = 16
MOBA_AHEAD = 1
MOBA_BUFFERS = 4

F32 = jnp.float32
BF16 = jnp.bfloat16


def _params(*sem):
    return pltpu.CompilerParams(dimension_semantics=sem, vmem_limit_bytes=VMEM_LIMIT)


def _const_spec(shape, index):
    return pl.BlockSpec(shape, lambda *_: index, pipeline_mode=pl.Buffered(1))


def _rms(x, g):
    return x * lax.rsqrt(jnp.mean(x * x, axis=-1, keepdims=True) + EPS) * g


def _dot(a, b):
    return jnp.dot(a, b, preferred_element_type=F32)


def _row_groups(n_rows, n_groups):
    size = n_rows // n_groups
    return [slice(i * size, (i + 1) * size) for i in range(n_groups)]


def _inproj_kernel(x_ref, g_ref, w_ref, qkv_ref, u_ref, *, attn_w, chunk):
    qkv_w = 3 * attn_w
    groups = _row_groups(x_ref.shape[0], INPROJ_SUBTILES)
    hs = [_rms(x_ref[rows, :], g_ref[...]).astype(BF16) for rows in groups]
    for rows, h in zip(groups, hs):
        for c in range(0, qkv_w, chunk):
            r = _dot(h, w_ref[:, c:c + chunk])
            if c < attn_w:
                r = r * (HEAD_DIM ** -0.5)
            qkv_ref[rows, c:c + chunk] = r.astype(BF16)
        u_ref[rows, :] = _dot(h, w_ref[:, qkv_w:])


def _inproj(x2, g, w, l, attn_w, u_w):
    m, d = x2.shape
    kern = functools.partial(_inproj_kernel, attn_w=attn_w, chunk=512)
    return pl.pallas_call(
        kern,
        grid=(m // TOKEN_TILE,),
        in_specs=[
            pl.BlockSpec((TOKEN_TILE, d), lambda i: (i, 0)),
            _const_spec((None, 1, d), (l, 0, 0)),
            _const_spec((None, d, 3 * attn_w + u_w), (l, 0, 0)),
        ],
        out_specs=[
            pl.BlockSpec((TOKEN_TILE, 3 * attn_w), lambda i: (i, 0)),
            pl.BlockSpec((TOKEN_TILE, u_w), lambda i: (i, 0)),
        ],
        out_shape=[
            jax.ShapeDtypeStruct((m, 3 * attn_w), BF16),
            jax.ShapeDtypeStruct((m, u_w), F32),
        ],
        compiler_params=_params("arbitrary"),
        name="inproj",
    )(x2, g, w)


def _moba_kernel(q_ref, k_ref, v_ref, b0_ref, b1_ref, cfar_ref, o_ref,
                 kmean_ref, vt_ref, *bufs, n_blocks):
    blk = MOBA_BLOCK
    for j in range(n_blocks):
        kb = k_ref[0, j * blk:(j + 1) * blk, :].astype(F32)
        kmean_ref[j:j + 1, :] = jnp.mean(kb, axis=0, keepdims=True)
        vt = v_ref[0, j * blk:(j + 1) * blk, :].astype(F32).T
        for h in range(HEADS_PER_STEP):
            vt_ref[h, 0:HEAD_DIM, j * blk:(j + 1) * blk] = (
                vt[h * HEAD_DIM:(h + 1) * HEAD_DIM].astype(BF16))
    for h in range(HEADS_PER_STEP):
        vt_ref[h, HEAD_DIM:HEAD_DIM + ONES_ROWS, :] = jnp.ones((ONES_ROWS, vt_ref.shape[2]), BF16)

    dim_row = lax.broadcasted_iota(jnp.int32, (LANES, blk), 0)
    blk_row = lax.broadcasted_iota(jnp.int32, (n_blocks, blk), 0)
    kmean = kmean_ref[...]

    n_buf = len(bufs) // 2
    s_refs, p_refs = bufs[:n_buf], bufs[n_buf:]
    qts = {}

    def pass1(qi, h, s_ref):
        if qi not in qts:
            qts[qi] = q_ref[0, qi * blk:(qi + 1) * blk, :].astype(F32).T
        in_head = (dim_row >= h * HEAD_DIM) & (dim_row < (h + 1) * HEAD_DIM)
        qt_h = jnp.where(in_head, qts[qi], 0.0)
        qt_hb = (qt_h * LOG2E).astype(BF16)

        gate = jnp.dot(kmean, qt_h, preferred_element_type=F32,
                       precision=lax.Precision.HIGHEST)
        rank = jnp.zeros((n_blocks, blk), jnp.int32)
        for j in range(qi):
            other = gate[j:j + 1, :]
            beats = (other > gate) | ((other == gate) & (j < blk_row))
            rank = rank + jnp.where(beats, 1, 0)
        sel = (blk_row < qi) & (rank < MOBA_TOPK)

        cfar = cfar_ref[h, 0:1, 0:1]
        col_max, col_bias = [], []
        n_far = max(qi - 1, 0)
        if n_far:
            r = _dot(k_ref[0, 0:n_far * blk, :], qt_hb)
            s_ref[0:n_far * blk, :] = r
            for j in range(n_far):
                col_max.append(jnp.max(r[j * blk:(j + 1) * blk], axis=0, keepdims=True) + cfar)
                col_bias.append(cfar)
        if qi >= 1:
            lo = (qi - 1) * blk
            r = _dot(k_ref[0, lo:lo + blk, :], qt_hb) + b1_ref[h]
            s_ref[lo:lo + blk, :] = r
            col_max.append(jnp.max(r, axis=0, keepdims=True))
            col_bias.append(None)
        lo = qi * blk
        r = _dot(k_ref[0, lo:lo + blk, :], qt_hb) + b0_ref[h]
        s_ref[lo:lo + blk, :] = r
        m = jnp.max(r, axis=0, keepdims=True)
        chosen = [sel[j:j + 1, :] for j in range(qi)]
        for j in range(qi):
            m = jnp.maximum(m, jnp.where(chosen[j], col_max[j], NEG))
        offs = []
        for j in range(qi):
            base = m if col_bias[j] is None else m - col_bias[j]
            offs.append(jnp.where(chosen[j], base, -NEG))
        return offs + [m]

    def pass2(qi, h, offs, s_ref, p_ref):
        for j in range(qi + 1):
            x = s_ref[j * blk:(j + 1) * blk, :] - offs[j]
            p_ref[j * blk:(j + 1) * blk, :] = jnp.exp2(x.astype(BF16))
        n_keys = (qi + 1) * blk
        acc = _dot(vt_ref[h, :, 0:n_keys], p_ref[0:n_keys, :])
        return acc[0:HEAD_DIM] / acc[HEAD_DIM:HEAD_DIM + 1]

    units = [(qi, h) for qi in range(n_blocks) for h in range(HEADS_PER_STEP)]
    offs, outs = {}, {}
    for n in range(len(units) + MOBA_AHEAD):
        if n < len(units):
            offs[n] = pass1(*units[n], s_refs[n % n_buf])
        if n >= MOBA_AHEAD:
            d = n - MOBA_AHEAD
            qi, h = units[d]
            outs[h] = pass2(qi, h, offs.pop(d), s_refs[d % n_buf], p_refs[d % n_buf])
            if h == HEADS_PER_STEP - 1:
                both = jnp.concatenate([outs[i] for i in range(HEADS_PER_STEP)], axis=0)
                o_ref[0, qi * blk:(qi + 1) * blk, :] = both.T.astype(o_ref.dtype)


def _moba(qkv3, bias0_t, bias1_t, cfar, attn_w):
    b, s, _ = qkv3.shape
    n_heads = attn_w // HEAD_DIM
    n_blocks = s // MOBA_BLOCK
    hp = n_heads // HEADS_PER_STEP
    cols = attn_w // LANES
    n_buf = MOBA_BUFFERS
    kern = functools.partial(_moba_kernel, n_blocks=n_blocks)
    bias_spec = pl.BlockSpec((HEADS_PER_STEP, MOBA_BLOCK, MOBA_BLOCK), lambda p, i: (p, 0, 0))
    return pl.pallas_call(
        kern,
        grid=(hp, b),
        in_specs=[
            pl.BlockSpec((1, s, LANES), lambda p, i: (i, 0, p)),
            pl.BlockSpec((1, s, LANES), lambda p, i: (i, 0, cols + p)),
            pl.BlockSpec((1, s, LANES), lambda p, i: (i, 0, 2 * cols + p)),
            bias_spec,
            bias_spec,
            pl.BlockSpec((HEADS_PER_STEP, 8, LANES), lambda p, i: (p, 0, 0)),
        ],
        out_specs=pl.BlockSpec((1, s, LANES), lambda p, i: (i, 0, p)),
        out_shape=jax.ShapeDtypeStruct((b, s, attn_w), BF16),
        scratch_shapes=[
            pltpu.VMEM((n_blocks, LANES), F32),
            pltpu.VMEM((HEADS_PER_STEP, HEAD_DIM + ONES_ROWS, s), BF16),
        ] + [pltpu.VMEM((s, MOBA_BLOCK), F32)] * n_buf + [pltpu.VMEM((s, MOBA_BLOCK), BF16)] * n_buf,
        compiler_params=_params("arbitrary", "arbitrary"),
        name="moba",
    )(qkv3, qkv3, qkv3, bias0_t, bias1_t, cfar)


def _t5_bucket(rel):
    n = jnp.maximum(rel, 0)
    max_exact = REL_BUCKETS // 2
    nf = jnp.maximum(n, 1).astype(F32)
    large = max_exact + (jnp.log(nf / max_exact) / math.log(REL_MAX_DIST / max_exact)
                         * (REL_BUCKETS - max_exact)).astype(jnp.int32)
    large = jnp.minimum(large, REL_BUCKETS - 1)
    return jnp.where(n < max_exact, n, large)


def _bias_tables(rel_bias):
    table_t = rel_bias.T.astype(F32)
    offs = jnp.arange(MOBA_BLOCK)
    rel_own = offs[None, :] - offs[:, None]
    buckets = jnp.arange(REL_BUCKETS)[:, None, None]

    def lookup(rel):
        onehot = (_t5_bucket(rel)[None] == buckets).astype(F32)
        return jnp.einsum('hb,bkq->hkq', table_t, onehot, precision=lax.Precision.HIGHEST)

    bias0 = jnp.where(rel_own >= 0, lookup(rel_own), NEG)
    bias1 = lookup(rel_own + MOBA_BLOCK)
    far = _t5_bucket(jnp.int32(2 * MOBA_BLOCK - (MOBA_BLOCK - 1)))
    cfar = jnp.broadcast_to(table_t[:, far][:, None, None], (table_t.shape[0], 8, LANES))
    return bias0 * LOG2E, bias1 * LOG2E, cfar * LOG2E


def _seq_kernel(u_ref, wg_ref, ps_ref, wdw_ref, bdw_ref, lng_ref, lnb_ref,
                pool_ref, conv_ref, rot_ref, *, pool_w, conv_w):
    s = u_ref.shape[1]
    group_w = pool_w // len(POOL_WINDOWS)

    def delayed(delay, t0, lanes):
        a8, r = delay // SUBLANES * SUBLANES, delay % SUBLANES
        return rot_ref[r, pl.ds(SEQ_PAD + t0 - a8, SEQ_TILE), lanes]

    def fill_delayed_copies(t0, lanes, n_copies):
        for r in range(1, n_copies):
            rot_ref[r, pl.ds(SEQ_PAD + t0, SEQ_TILE), lanes] = (
                rot_ref[0, pl.ds(SEQ_PAD + t0 - r, SEQ_TILE), lanes])

    for r in range(SUBLANES):
        rot_ref[r, 0:SEQ_PAD, :] = jnp.zeros((SEQ_PAD, rot_ref.shape[2]), F32)

    lane_tiles = []
    for lo in range(0, pool_w, LANES):
        windows = [w for gi, w in enumerate(POOL_WINDOWS) if lo <= gi * group_w < lo + LANES]
        lane_group = (lax.broadcasted_iota(jnp.int32, (SEQ_TILE, LANES), 1) + lo) // group_w
        window = jnp.zeros((SEQ_TILE, LANES), F32)
        for gi, w in enumerate(POOL_WINDOWS):
            window = jnp.where(lane_group == gi, float(w), window)
        lane_tiles.append((slice(lo, lo + LANES), windows, window))
    row = lax.broadcasted_iota(jnp.int32, (SEQ_TILE, LANES), 0)

    for t0 in range(0, s, SEQ_TILE):
        rot_ref[0, pl.ds(SEQ_PAD + t0, SEQ_TILE), 0:pool_w] = u_ref[0, pl.ds(t0, SEQ_TILE), 0:pool_w]
        for lanes, windows, _ in lane_tiles:
            fill_delayed_copies(t0, lanes, min(SUBLANES, max(windows)))
    for t0 in range(0, s, SEQ_TILE):
        parts = []
        for lanes, windows, window in lane_tiles:
            tok = delayed(0, t0, lanes)
            run = tok
            wsum = jnp.zeros_like(tok)
            for k in range(1, max(windows)):
                run = run + delayed(k, t0, lanes)
                if k + 1 in windows:
                    wsum = jnp.where(window == float(k + 1), run, wsum)
            count = jnp.minimum((row + t0 + 1).astype(F32), window)
            parts.append(wsum / count - tok)
        d = jnp.concatenate(parts, axis=1)
        y = _dot(d.astype(BF16), wg_ref[...]) * ps_ref[...]
        pool_ref[0, pl.ds(t0, SEQ_TILE), :] = y.astype(pool_ref.dtype)

    conv_lanes = slice(0, conv_w)
    for t0 in range(0, s, SEQ_TILE):
        rows = pl.ds(t0, SEQ_TILE)
        a = u_ref[0, rows, pool_w:pool_w + conv_w]
        gt = u_ref[0, rows, pool_w + conv_w:pool_w + 2 * conv_w]
        rot_ref[0, pl.ds(SEQ_PAD + t0, SEQ_TILE), conv_lanes] = a * jax.nn.sigmoid(gt)
        fill_delayed_copies(t0, conv_lanes, SUBLANES)
    for t0 in range(0, s, SEQ_TILE):
        acc = jnp.zeros((SEQ_TILE, conv_w), F32) + bdw_ref[...]
        for k in range(CONV_KERNEL):
            acc = acc + delayed(CONV_KERNEL - 1 - k, t0, conv_lanes) * wdw_ref[k:k + 1, :]
        mu = jnp.mean(acc, axis=-1, keepdims=True)
        cen = acc - mu
        var = jnp.mean(cen * cen, axis=-1, keepdims=True)
        yn = cen * lax.rsqrt(var + EPS) * lng_ref[...] + lnb_ref[...]
        conv_ref[0, pl.ds(t0, SEQ_TILE), :] = (yn * jax.nn.sigmoid(yn)).astype(conv_ref.dtype)


def _seq_mixers(u3, wg_bd, pool_scale, w_dw, b_dw, ln_g, ln_b, l, pool_w, conv_w):
    b, s, u_w = u3.shape
    kern = functools.partial(_seq_kernel, pool_w=pool_w, conv_w=conv_w)
    return pl.pallas_call(
        kern,
        grid=(b,),
        in_specs=[
            pl.BlockSpec((1, s, u_w), lambda i: (i, 0, 0)),
            _const_spec((None, pool_w, pool_w), (l, 0, 0)),
            _const_spec((None, 1, pool_w), (l, 0, 0)),
            _const_spec((None, CONV_KERNEL, conv_w), (l, 0, 0)),
            _const_spec((None, 1, conv_w), (l, 0, 0)),
            _const_spec((None, 1, conv_w), (l, 0, 0)),
            _const_spec((None, 1, conv_w), (l, 0, 0)),
        ],
        out_specs=[
            pl.BlockSpec((1, s, pool_w), lambda i: (i, 0, 0)),
            pl.BlockSpec((1, s, conv_w), lambda i: (i, 0, 0)),
        ],
        out_shape=[
            jax.ShapeDtypeStruct((b, s, pool_w), BF16),
            jax.ShapeDtypeStruct((b, s, conv_w), BF16),
        ],
        scratch_shapes=[pltpu.VMEM((SUBLANES, SEQ_PAD + s, max(pool_w, conv_w)), F32)],
        compiler_params=_params("arbitrary"),
        name="seq_mixers",
    )(u3, wg_bd, pool_scale, w_dw, b_dw, ln_g, ln_b)


def _merge_kernel(x_ref, attn_ref, pool_ref, conv_ref, gpre_ref, wg_ref, bg_ref, wa_ref, wp_ref,
                  wc_ref, wo_ref, gpost_ref, o_ref):
    d = x_ref.shape[1]
    branches = ((attn_ref, wa_ref), (pool_ref, wp_ref), (conv_ref, wc_ref))
    groups = _row_groups(x_ref.shape[0], MERGE_SUBTILES)
    xs = [x_ref[rows, :] for rows in groups]
    hs = [_rms(x, gpre_ref[...]).astype(BF16) for x in xs]
    ys = []
    for rows, h in zip(groups, hs):
        merged = None
        for i, (act_ref, w_ref) in enumerate(branches):
            cols = slice(i * d, (i + 1) * d)
            gate = jax.nn.sigmoid(_dot(h, wg_ref[0, :, cols]) + bg_ref[:, cols])
            part = gate * _dot(act_ref[rows, :], w_ref[...])
            merged = part if merged is None else merged + part
        ys.append(_dot(merged.astype(BF16), wo_ref[...]))
    for rows, x, y in zip(groups, xs, ys):
        o_ref[rows, :] = x + _rms(y, gpost_ref[...])


def _merge(x2, attn, pool, conv, gpre, w_in, bg, wa, wp, wc, wo, gpost, l):
    m, d = x2.shape
    gate_w = bg.shape[2]
    tile = lambda w: pl.BlockSpec((TOKEN_TILE, w), lambda i: (i, 0))
    layer = lambda a: _const_spec((None,) + a.shape[1:], (l, 0, 0))
    gate_cols = pl.BlockSpec((pl.Element(1), pl.Element(d), pl.Element(gate_w)),
                             lambda i: (l, 0, w_in.shape[2] - gate_w), pipeline_mode=pl.Buffered(1))
    return pl.pallas_call(
        _merge_kernel,
        grid=(m // TOKEN_TILE,),
        in_specs=[
            tile(d), tile(attn.shape[1]), tile(pool.shape[1]), tile(conv.shape[1]),
            layer(gpre), gate_cols, layer(bg), layer(wa), layer(wp), layer(wc), layer(wo),
            layer(gpost),
        ],
        out_specs=tile(d),
        out_shape=jax.ShapeDtypeStruct((m, d), F32),
        compiler_params=_params("arbitrary"),
        name="merge",
    )(x2, attn, pool, conv, gpre, w_in, bg, wa, wp, wc, wo, gpost)


def _ffn_kernel(x_ref, gpre_ref, win_ref, wout_ref, gpost_ref, o_ref, *, d_ff):
    groups = _row_groups(x_ref.shape[0], FFN_SUBTILES)
    xs = [x_ref[rows, :] for rows in groups]
    hs = [_rms(x, gpre_ref[...]).astype(BF16) for x in xs]
    fs = []
    for h in hs:
        f = None
        for c in range(0, d_ff, FFN_CHUNK):
            w = min(FFN_CHUNK, d_ff - c)
            gt = _dot(h, win_ref[:, c:c + w])
            up = _dot(h, win_ref[:, d_ff + c:d_ff + c + w])
            act = (gt * jax.nn.sigmoid(gt) * up).astype(BF16)
            part = _dot(act, wout_ref[c:c + w, :])
            f = part if f is None else f + part
        fs.append(f)
    for rows, x, f in zip(groups, xs, fs):
        o_ref[rows, :] = x + _rms(f, gpost_ref[...])


def _ffn(x2, gpre, win, wout, gpost, l):
    m, d = x2.shape
    d_ff = wout.shape[1]
    kern = functools.partial(_ffn_kernel, d_ff=d_ff)
    return pl.pallas_call(
        kern,
        grid=(m // TOKEN_TILE,),
        in_specs=[
            pl.BlockSpec((TOKEN_TILE, d), lambda i: (i, 0)),
            _const_spec((None, 1, d), (l, 0, 0)),
            _const_spec((None, d, 2 * d_ff), (l, 0, 0)),
            _const_spec((None, d_ff, d), (l, 0, 0)),
            _const_spec((None, 1, d), (l, 0, 0)),
        ],
        out_specs=pl.BlockSpec((TOKEN_TILE, d), lambda i: (i, 0)),
        out_shape=jax.ShapeDtypeStruct((m, d), F32),
        compiler_params=_params("arbitrary"),
        name="ffn",
    )(x2, gpre, win, wout, gpost)


def _block_diag(w):
    depth, groups, gd, _ = w.shape
    eye = jnp.eye(groups, dtype=w.dtype)
    return jnp.einsum('lgcd,gh->lgchd', w, eye).reshape(depth, groups * gd, groups * gd)


def kernel(x, pre_mix_g, w_in, b_gate, w_attn_o, rel_bias, w_pool_g, pool_scale, w_pool_o, w_dw,
           b_dw, conv_ln_g, conv_ln_b, w_conv_o, w_out, post_mix_g, pre_ffn_g, w_ffn_in,
           w_ffn_out, post_ffn_g):
    b, s, d = x.shape
    depth = w_in.shape[0]
    attn_w = w_attn_o.shape[1]
    pool_w = w_pool_o.shape[1]
    conv_w = w_conv_o.shape[1]
    u_w = pool_w + 2 * conv_w

    row = lambda a: a.reshape(depth, 1, a.shape[-1])
    w_in_b = w_in.astype(BF16)
    wa_b, wp_b, wc_b, wo_b = (w.astype(BF16) for w in (w_attn_o, w_pool_o, w_conv_o, w_out))
    wfi_b, wfo_b = w_ffn_in.astype(BF16), w_ffn_out.astype(BF16)
    wg_bd = _block_diag(w_pool_g).astype(BF16)
    bias0_t, bias1_t, cfar = _bias_tables(rel_bias)

    x2 = x.reshape(b * s, d)
    for l in range(depth):
        qkv, u = _inproj(x2, row(pre_mix_g), w_in_b, l, attn_w, u_w)
        attn = _moba(qkv.reshape(b, s, 3 * attn_w), bias0_t, bias1_t, cfar, attn_w)
        pool, conv = _seq_mixers(u.reshape(b, s, u_w), wg_bd, row(pool_scale), w_dw, row(b_dw),
                                 row(conv_ln_g), row(conv_ln_b), l, pool_w, conv_w)
        x2 = _merge(x2, attn.reshape(b * s, attn_w), pool.reshape(b * s, pool_w),
                    conv.reshape(b * s, conv_w), row(pre_mix_g), w_in_b, row(b_gate),
                    wa_b, wp_b, wc_b, wo_b, row(post_mix_g), l)
        x2 = _ffn(x2, row(pre_ffn_g), wfi_b, wfo_b, row(post_ffn_g), l)
    return x2.reshape(b, s, d)
```

```python
import functools
import math

import jax
import jax.numpy as jnp
from jax import lax
from jax.experimental import pallas as pl
from jax.experimental.pallas import tpu as pltpu

HEAD_DIM = 64
MOBA_BLOCK = 256
MOBA_TOPK = 3
POOL_WINDOWS = (2, 4, 8, 16)
CONV_KERNEL = 31
REL_BUCKETS = 32
REL_MAX_DIST = 128
EPS = 1e-6
NEG = -1e30
LOG2E = math.log2(math.e)

LANES = 128
HEADS_PER_STEP = LANES // HEAD_DIM
TOKEN_TILE = 1024
INPROJ_SUBTILES = 2
MERGE_TILE = 512
FFN_SUBTILES = 4
FFN_CHUNK = 1024
VMEM_LIMIT = 56 * 1024 * 1024
SUBLANES = 8
SEQ_PAD = 32
SEQ_TILE = 256
ONES_ROWS = 16
MOBA_AHEAD = 1
MOBA_BUFFERS = 4

F32 = jnp.float32
BF16 = jnp.bfloat16


def _params(*sem):
    return pltpu.CompilerParams(dimension_semantics=sem, vmem_limit_bytes=VMEM_LIMIT)


def _const_spec(shape, index):
    return pl.BlockSpec(shape, lambda *_: index, pipeline_mode=pl.Buffered(1))


def _rms(x, g):
    return x * lax.rsqrt(jnp.mean(x * x, axis=-1, keepdims=True) + EPS) * g


def _dot(a, b):
    return jnp.dot(a, b, preferred_element_type=F32)


def _row_groups(n_rows, n_groups):
    size = n_rows // n_groups
    return [slice(i * size, (i + 1) * size) for i in range(n_groups)]


def _inproj_kernel(x_ref, g_ref, w_ref, qkv_ref, u_ref, *, attn_w, chunk):
    qkv_w = 3 * attn_w
    groups = _row_groups(x_ref.shape[0], INPROJ_SUBTILES)
    hs = [_rms(x_ref[rows, :], g_ref[...]).astype(BF16) for rows in groups]
    for rows, h in zip(groups, hs):
        for c in range(0, qkv_w, chunk):
            r = _dot(h, w_ref[:, c:c + chunk])
            if c < attn_w:
                r = r * (HEAD_DIM ** -0.5)
            qkv_ref[rows, c:c + chunk] = r.astype(BF16)
        u_ref[rows, :] = _dot(h, w_ref[:, qkv_w:])


def _inproj(x2, g, w, l, attn_w, u_w):
    m, d = x2.shape
    kern = functools.partial(_inproj_kernel, attn_w=attn_w, chunk=512)
    return pl.pallas_call(
        kern,
        grid=(m // TOKEN_TILE,),
        in_specs=[
            pl.BlockSpec((TOKEN_TILE, d), lambda i: (i, 0)),
            _const_spec((None, 1, d), (l, 0, 0)),
            _const_spec((None, d, 3 * attn_w + u_w), (l, 0, 0)),
        ],
        out_specs=[
            pl.BlockSpec((TOKEN_TILE, 3 * attn_w), lambda i: (i, 0)),
            pl.BlockSpec((TOKEN_TILE, u_w), lambda i: (i, 0)),
        ],
        out_shape=[
            jax.ShapeDtypeStruct((m, 3 * attn_w), BF16),
            jax.ShapeDtypeStruct((m, u_w), F32),
        ],
        compiler_params=_params("arbitrary"),
        name="inproj",
    )(x2, g, w)


def _moba_kernel(q_ref, k_ref, v_ref, b0_ref, b1_ref, cfar_ref, o_ref,
                 kmean_ref, vt_ref, *bufs, n_blocks):
    blk = MOBA_BLOCK
    for j in range(n_blocks):
        kb = k_ref[0, j * blk:(j + 1) * blk, :].astype(F32)
        kmean_ref[j:j + 1, :] = jnp.mean(kb, axis=0, keepdims=True)
        vt = v_ref[0, j * blk:(j + 1) * blk, :].astype(F32).T
        for h in range(HEADS_PER_STEP):
            vt_ref[h, 0:HEAD_DIM, j * blk:(j + 1) * blk] = (
                vt[h * HEAD_DIM:(h + 1) * HEAD_DIM].astype(BF16))
    for h in range(HEADS_PER_STEP):
        vt_ref[h, HEAD_DIM:HEAD_DIM + ONES_ROWS, :] = jnp.ones((ONES_ROWS, vt_ref.shape[2]), BF16)

    dim_row = lax.broadcasted_iota(jnp.int32, (LANES, blk), 0)
    blk_row = lax.broadcasted_iota(jnp.int32, (n_blocks, blk), 0)
    kmean = kmean_ref[...]

    n_buf = len(bufs) // 2
    s_refs, p_refs = bufs[:n_buf], bufs[n_buf:]
    qts = {}

    def pass1(qi, h, s_ref):
        if qi not in qts:
            qts[qi] = q_ref[0, qi * blk:(qi + 1) * blk, :].astype(F32).T
        in_head = (dim_row >= h * HEAD_DIM) & (dim_row < (h + 1) * HEAD_DIM)
        qt_h = jnp.where(in_head, qts[qi], 0.0)
        qt_hb = (qt_h * LOG2E).astype(BF16)

        gate = jnp.dot(kmean, qt_h, preferred_element_type=F32,
                       precision=lax.Precision.HIGHEST)
        rank = jnp.zeros((n_blocks, blk), jnp.int32)
        for j in range(qi):
            other = gate[j:j + 1, :]
            beats = (other > gate) | ((other == gate) & (j < blk_row))
            rank = rank + jnp.where(beats, 1, 0)
        sel = (blk_row < qi) & (rank < MOBA_TOPK)

        cfar = cfar_ref[h, 0:1, 0:1]
        col_max, col_bias = [], []
        n_far = max(qi - 1, 0)
        if n_far:
            r = _dot(k_ref[0, 0:n_far * blk, :], qt_hb)
            s_ref[0:n_far * blk, :] = r
            for j in range(n_far):
                col_max.append(jnp.max(r[j * blk:(j + 1) * blk], axis=0, keepdims=True) + cfar)
                col_bias.append(cfar)
        if qi >= 1:
            lo = (qi - 1) * blk
            r = _dot(k_ref[0, lo:lo + blk, :], qt_hb) + b1_ref[h]
            s_ref[lo:lo + blk, :] = r
            col_max.append(jnp.max(r, axis=0, keepdims=True))
            col_bias.append(None)
        lo = qi * blk
        r = _dot(k_ref[0, lo:lo + blk, :], qt_hb) + b0_ref[h]
        s_ref[lo:lo + blk, :] = r
        m = jnp.max(r, axis=0, keepdims=True)
        chosen = [sel[j:j + 1, :] for j in range(qi)]
        for j in range(qi):
            m = jnp.maximum(m, jnp.where(chosen[j], col_max[j], NEG))
        offs = []
        for j in range(qi):
            base = m if col_bias[j] is None else m - col_bias[j]
            offs.append(jnp.where(chosen[j], base, -NEG))
        return offs + [m]

    def pass2(qi, h, offs, s_ref, p_ref):
        for j in range(qi + 1):
            x = s_ref[j * blk:(j + 1) * blk, :] - offs[j]
            p_ref[j * blk:(j + 1) * blk, :] = jnp.exp2(x.astype(BF16))
        n_keys = (qi + 1) * blk
        acc = _dot(vt_ref[h, :, 0:n_keys], p_ref[0:n_keys, :])
        return acc[0:HEAD_DIM] / acc[HEAD_DIM:HEAD_DIM + 1]

    units = [(qi, h) for qi in range(n_blocks) for h in range(HEADS_PER_STEP)]
    offs, outs = {}, {}
    for n in range(len(units) + MOBA_AHEAD):
        if n < len(units):
            offs[n] = pass1(*units[n], s_refs[n % n_buf])
        if n >= MOBA_AHEAD:
            d = n - MOBA_AHEAD
            qi, h = units[d]
            outs[h] = pass2(qi, h, offs.pop(d), s_refs[d % n_buf], p_refs[d % n_buf])
            if h == HEADS_PER_STEP - 1:
                both = jnp.concatenate([outs[i] for i in range(HEADS_PER_STEP)], axis=0)
                o_ref[0, qi * blk:(qi + 1) * blk, :] = both.T.astype(o_ref.dtype)


def _moba(qkv3, bias0_t, bias1_t, cfar, attn_w):
    b, s, _ = qkv3.shape
    n_heads = attn_w // HEAD_DIM
    n_blocks = s // MOBA_BLOCK
    hp = n_heads // HEADS_PER_STEP
    cols = attn_w // LANES
    n_buf = MOBA_BUFFERS
    kern = functools.partial(_moba_kernel, n_blocks=n_blocks)
    bias_spec = pl.BlockSpec((HEADS_PER_STEP, MOBA_BLOCK, MOBA_BLOCK), lambda p, i: (p, 0, 0))
    return pl.pallas_call(
        kern,
        grid=(hp, b),
        in_specs=[
            pl.BlockSpec((1, s, LANES), lambda p, i: (i, 0, p)),
            pl.BlockSpec((1, s, LANES), lambda p, i: (i, 0, cols + p)),
            pl.BlockSpec((1, s, LANES), lambda p, i: (i, 0, 2 * cols + p)),
            bias_spec,
            bias_spec,
            pl.BlockSpec((HEADS_PER_STEP, 8, LANES), lambda p, i: (p, 0, 0)),
        ],
        out_specs=pl.BlockSpec((1, s, LANES), lambda p, i: (i, 0, p)),
        out_shape=jax.ShapeDtypeStruct((b, s, attn_w), BF16),
        scratch_shapes=[
            pltpu.VMEM((n_blocks, LANES), F32),
            pltpu.VMEM((HEADS_PER_STEP, HEAD_DIM + ONES_ROWS, s), BF16),
        ] + [pltpu.VMEM((s, MOBA_BLOCK), F32)] * n_buf + [pltpu.VMEM((s, MOBA_BLOCK), BF16)] * n_buf,
        compiler_params=_params("arbitrary", "arbitrary"),
        name="moba",
    )(qkv3, qkv3, qkv3, bias0_t, bias1_t, cfar)


def _t5_bucket(rel):
    n = jnp.maximum(rel, 0)
    max_exact = REL_BUCKETS // 2
    nf = jnp.maximum(n, 1).astype(F32)
    large = max_exact + (jnp.log(nf / max_exact) / math.log(REL_MAX_DIST / max_exact)
                         * (REL_BUCKETS - max_exact)).astype(jnp.int32)
    large = jnp.minimum(large, REL_BUCKETS - 1)
    return jnp.where(n < max_exact, n, large)


def _bias_tables(rel_bias):
    table_t = rel_bias.T.astype(F32)
    offs = jnp.arange(MOBA_BLOCK)
    rel_own = offs[None, :] - offs[:, None]
    buckets = jnp.arange(REL_BUCKETS)[:, None, None]

    def lookup(rel):
        onehot = (_t5_bucket(rel)[None] == buckets).astype(F32)
        return jnp.einsum('hb,bkq->hkq', table_t, onehot, precision=lax.Precision.HIGHEST)

    bias0 = jnp.where(rel_own >= 0, lookup(rel_own), NEG)
    bias1 = lookup(rel_own + MOBA_BLOCK)
    far = _t5_bucket(jnp.int32(2 * MOBA_BLOCK - (MOBA_BLOCK - 1)))
    cfar = jnp.broadcast_to(table_t[:, far][:, None, None], (table_t.shape[0], 8, LANES))
    return bias0 * LOG2E, bias1 * LOG2E, cfar * LOG2E


def _pool_lane_tiles(pool_w):
    group_w = pool_w // len(POOL_WINDOWS)
    tiles = []
    for lo in range(0, pool_w, LANES):
        windows = [w for gi, w in enumerate(POOL_WINDOWS) if lo <= gi * group_w < lo + LANES]
        lane_group = (lax.broadcasted_iota(jnp.int32, (SEQ_TILE, LANES), 1) + lo) // group_w
        window = jnp.zeros((SEQ_TILE, LANES), F32)
        for gi, w in enumerate(POOL_WINDOWS):
            window = jnp.where(lane_group == gi, float(w), window)
        tiles.append((slice(lo, lo + LANES), windows, window))
    return tiles


def _merge_kernel(x_ref, attn_ref, u_ref, gpre_ref, wg_ref, bg_ref, wa_ref, wp_ref, wc_ref, wo_ref,
                  gpost_ref, wpg_ref, ps_ref, wdw_ref, bdw_ref, lng_ref, lnb_ref, o_ref,
                  prot_ref, crot_ref, *, tiles_per_seq, pool_w, conv_w):
    d = x_ref.shape[1]
    n_rows = x_ref.shape[0]
    step = pl.program_id(0)
    tile_in_seq = step % tiles_per_seq
    seq_start = tile_in_seq == 0
    lane_tiles = _pool_lane_tiles(pool_w)
    conv_lanes = slice(0, conv_w)
    copies = [(prot_ref, lanes, min(SUBLANES, max(windows))) for lanes, windows, _ in lane_tiles]
    copies.append((crot_ref, conv_lanes, SUBLANES))

    @pl.when(step == 0)
    def _first_halo_source():
        for ref, lanes, n_copies in copies:
            for r in range(n_copies):
                ref[r, n_rows:n_rows + SEQ_PAD, lanes] = jnp.zeros((SEQ_PAD, lanes.stop - lanes.start), F32)

    for ref, lanes, n_copies in copies:
        for r in range(n_copies):
            tail = ref[r, n_rows:n_rows + SEQ_PAD, lanes]
            ref[r, 0:SEQ_PAD, lanes] = jnp.where(seq_start, 0.0, tail)

    def delayed(ref, delay, t0, lanes):
        a8, r = delay // SUBLANES * SUBLANES, delay % SUBLANES
        return ref[r, pl.ds(SEQ_PAD + t0 - a8, SEQ_TILE), lanes]

    for t0 in range(0, n_rows, SEQ_TILE):
        rows = pl.ds(t0, SEQ_TILE)
        here = pl.ds(SEQ_PAD + t0, SEQ_TILE)
        prot_ref[0, here, 0:pool_w] = u_ref[rows, 0:pool_w]
        a = u_ref[rows, pool_w:pool_w + conv_w]
        gt = u_ref[rows, pool_w + conv_w:pool_w + 2 * conv_w]
        crot_ref[0, here, conv_lanes] = a * jax.nn.sigmoid(gt)
        for ref, lanes, n_copies in copies:
            for r in range(1, n_copies):
                ref[r, here, lanes] = ref[0, pl.ds(SEQ_PAD + t0 - r, SEQ_TILE), lanes]

    row = lax.broadcasted_iota(jnp.int32, (SEQ_TILE, LANES), 0)

    def pool_branch(t0):
        parts = []
        for lanes, windows, window in lane_tiles:
            tok = delayed(prot_ref, 0, t0, lanes)
            run = tok
            wsum = jnp.zeros_like(tok)
            for k in range(1, max(windows)):
                run = run + delayed(prot_ref, k, t0, lanes)
                if k + 1 in windows:
                    wsum = jnp.where(window == float(k + 1), run, wsum)
            seen = (row + (tile_in_seq * n_rows + t0 + 1)).astype(F32)
            parts.append(wsum / jnp.minimum(seen, window) - tok)
        y = _dot(jnp.concatenate(parts, axis=1).astype(BF16), wpg_ref[...]) * ps_ref[...]
        return y.astype(BF16)

    def conv_branch(t0):
        acc = jnp.zeros((SEQ_TILE, conv_w), F32) + bdw_ref[...]
        for k in range(CONV_KERNEL):
            acc = acc + delayed(crot_ref, CONV_KERNEL - 1 - k, t0, conv_lanes) * wdw_ref[k:k + 1, :]
        mu = jnp.mean(acc, axis=-1, keepdims=True)
        cen = acc - mu
        var = jnp.mean(cen * cen, axis=-1, keepdims=True)
        yn = cen * lax.rsqrt(var + EPS) * lng_ref[...] + lnb_ref[...]
        return (yn * jax.nn.sigmoid(yn)).astype(BF16)

    groups = [slice(t0, t0 + SEQ_TILE) for t0 in range(0, n_rows, SEQ_TILE)]
    xs = [x_ref[rows, :] for rows in groups]
    hs = [_rms(x, gpre_ref[...]).astype(BF16) for x in xs]
    ys = []
    for rows, h in zip(groups, hs):
        acts = (attn_ref[rows, :], pool_branch(rows.start), conv_branch(rows.start))
        merged = None
        for i, (act, w_ref) in enumerate(zip(acts, (wa_ref, wp_ref, wc_ref))):
            cols = slice(i * d, (i + 1) * d)
            gate = jax.nn.sigmoid(_dot(h, wg_ref[0, :, cols]) + bg_ref[:, cols])
            part = gate * _dot(act, w_ref[...])
            merged = part if merged is None else merged + part
        ys.append(_dot(merged.astype(BF16), wo_ref[...]))
    for rows, x, y in zip(groups, xs, ys):
        o_ref[rows, :] = x + _rms(y, gpost_ref[...])


def _merge(x2, attn, u, seq_len, gpre, w_in, bg, wa, wp, wc, wo, gpost, wpg, ps, wdw, bdw, lng, lnb, l):
    m, d = x2.shape
    gate_w = bg.shape[2]
    pool_w, conv_w = wp.shape[1], wc.shape[1]
    tile = lambda w: pl.BlockSpec((MERGE_TILE, w), lambda i: (i, 0))
    layer = lambda a: _const_spec((None,) + a.shape[1:], (l, 0, 0))
    gate_cols = pl.BlockSpec((pl.Element(1), pl.Element(d), pl.Element(gate_w)),
                             lambda i: (l, 0, w_in.shape[2] - gate_w), pipeline_mode=pl.Buffered(1))
    kern = functools.partial(_merge_kernel, tiles_per_seq=seq_len // MERGE_TILE,
                             pool_w=pool_w, conv_w=conv_w)
    return pl.pallas_call(
        kern,
        grid=(m // MERGE_TILE,),
        in_specs=[
            tile(d), tile(attn.shape[1]), tile(u.shape[1]),
            layer(gpre), gate_cols, layer(bg), layer(wa), layer(wp), layer(wc), layer(wo),
            layer(gpost), layer(wpg), layer(ps), layer(wdw), layer(bdw), layer(lng), layer(lnb),
        ],
        out_specs=tile(d),
        out_shape=jax.ShapeDtypeStruct((m, d), F32),
        scratch_shapes=[
            pltpu.VMEM((SUBLANES, SEQ_PAD + MERGE_TILE + SEQ_PAD, pool_w), F32),
            pltpu.VMEM((SUBLANES, SEQ_PAD + MERGE_TILE + SEQ_PAD, conv_w), F32),
        ],
        compiler_params=_params("arbitrary"),
        name="merge",
    )(x2, attn, u, gpre, w_in, bg, wa, wp, wc, wo, gpost, wpg, ps, wdw, bdw, lng, lnb)


def _ffn_kernel(x_ref, gpre_ref, win_ref, wout_ref, gpost_ref, o_ref, *, d_ff):
    groups = _row_groups(x_ref.shape[0], FFN_SUBTILES)
    xs = [x_ref[rows, :] for rows in groups]
    hs = [_rms(x, gpre_ref[...]).astype(BF16) for x in xs]
    fs = []
    for h in hs:
        f = None
        for c in range(0, d_ff, FFN_CHUNK):
            w = min(FFN_CHUNK, d_ff - c)
            gt = _dot(h, win_ref[:, c:c + w])
            up = _dot(h, win_ref[:, d_ff + c:d_ff + c + w])
            act = (gt * jax.nn.sigmoid(gt) * up).astype(BF16)
            part = _dot(act, wout_ref[c:c + w, :])
            f = part if f is None else f + part
        fs.append(f)
    for rows, x, f in zip(groups, xs, fs):
        o_ref[rows, :] = x + _rms(f, gpost_ref[...])


def _ffn(x2, gpre, win, wout, gpost, l):
    m, d = x2.shape
    d_ff = wout.shape[1]
    kern = functools.partial(_ffn_kernel, d_ff=d_ff)
    return pl.pallas_call(
        kern,
        grid=(m // TOKEN_TILE,),
        in_specs=[
            pl.BlockSpec((TOKEN_TILE, d), lambda i: (i, 0)),
            _const_spec((None, 1, d), (l, 0, 0)),
            _const_spec((None, d, 2 * d_ff), (l, 0, 0)),
            _const_spec((None, d_ff, d), (l, 0, 0)),
            _const_spec((None, 1, d), (l, 0, 0)),
        ],
        out_specs=pl.BlockSpec((TOKEN_TILE, d), lambda i: (i, 0)),
        out_shape=jax.ShapeDtypeStruct((m, d), F32),
        compiler_params=_params("arbitrary"),
        name="ffn",
    )(x2, gpre, win, wout, gpost)


def _block_diag(w):
    depth, groups, gd, _ = w.shape
    eye = jnp.eye(groups, dtype=w.dtype)
    return jnp.einsum('lgcd,gh->lgchd', w, eye).reshape(depth, groups * gd, groups * gd)


def kernel(x, pre_mix_g, w_in, b_gate, w_attn_o, rel_bias, w_pool_g, pool_scale, w_pool_o, w_dw,
           b_dw, conv_ln_g, conv_ln_b, w_conv_o, w_out, post_mix_g, pre_ffn_g, w_ffn_in,
           w_ffn_out, post_ffn_g):
    b, s, d = x.shape
    depth = w_in.shape[0]
    attn_w = w_attn_o.shape[1]
    pool_w = w_pool_o.shape[1]
    conv_w = w_conv_o.shape[1]
    u_w = pool_w + 2 * conv_w

    row = lambda a: a.reshape(depth, 1, a.shape[-1])
    w_in_b = w_in.astype(BF16)
    wa_b, wp_b, wc_b, wo_b = (w.astype(BF16) for w in (w_attn_o, w_pool_o, w_conv_o, w_out))
    wfi_b, wfo_b = w_ffn_in.astype(BF16), w_ffn_out.astype(BF16)
    wg_bd = _block_diag(w_pool_g).astype(BF16)
    bias0_t, bias1_t, cfar = _bias_tables(rel_bias)

    x2 = x.reshape(b * s, d)
    for l in range(depth):
        qkv, u = _inproj(x2, row(pre_mix_g), w_in_b, l, attn_w, u_w)
        attn = _moba(qkv.reshape(b, s, 3 * attn_w), bias0_t, bias1_t, cfar, attn_w)
        x2 = _merge(x2, attn.reshape(b * s, attn_w), u, s, row(pre_mix_g), w_in_b, row(b_gate),
                    wa_b, wp_b, wc_b, wo_b, row(post_mix_g), wg_bd, row(pool_scale), w_dw,
                    row(b_dw), row(conv_ln_g), row(conv_ln_b), l)
        x2 = _ffn(x2, row(pre_ffn_g), wfi_b, wfo_b, row(post_ffn_g), l)
    return x2.reshape(b, s, d)
```

```python
import functools
import math

import jax
import jax.numpy as jnp
from jax import lax
from jax.experimental import pallas as pl
from jax.experimental.pallas import tpu as pltpu

HEAD_DIM = 64
MOBA_BLOCK = 256
MOBA_TOPK = 3
POOL_WINDOWS = (2, 4, 8, 16)
CONV_KERNEL = 31
REL_BUCKETS = 32
REL_MAX_DIST = 128
EPS = 1e-6
NEG = -1e30
LOG2E = math.log2(math.e)

LANES = 128
HEADS_PER_STEP = LANES // HEAD_DIM
TOKEN_TILE = 1024
INPROJ_SUBTILES = 2
MERGE_TILE = 512
FFN_SUBTILES = 4
FFN_CHUNK = 1024
VMEM_LIMIT = 56 * 1024 * 1024
SUBLANES = 8
SEQ_PAD = 32
SEQ_TILE = 256
PACK = 16
ONES_ROWS = PACK
MOBA_AHEAD = 1
MOBA_BUFFERS = 4

F32 = jnp.float32
BF16 = jnp.bfloat16


def _params(*sem):
    return pltpu.CompilerParams(dimension_semantics=sem, vmem_limit_bytes=VMEM_LIMIT)


def _const_spec(shape, index):
    return pl.BlockSpec(shape, lambda *_: index, pipeline_mode=pl.Buffered(1))


def _rms(x, g):
    return x * lax.rsqrt(jnp.mean(x * x, axis=-1, keepdims=True) + EPS) * g


def _dot(a, b):
    return jnp.dot(a, b, preferred_element_type=F32)


def _row_groups(n_rows, n_groups):
    size = n_rows // n_groups
    return [slice(i * size, (i + 1) * size) for i in range(n_groups)]


def _inproj_kernel(x_ref, g_ref, w_ref, qkv_ref, u_ref, *, attn_w, chunk):
    qkv_w = 3 * attn_w
    groups = _row_groups(x_ref.shape[0], INPROJ_SUBTILES)
    hs = [_rms(x_ref[rows, :], g_ref[...]).astype(BF16) for rows in groups]
    for rows, h in zip(groups, hs):
        for c in range(0, qkv_w, chunk):
            r = _dot(h, w_ref[:, c:c + chunk])
            if c < attn_w:
                r = r * (HEAD_DIM ** -0.5)
            qkv_ref[rows, c:c + chunk] = r.astype(BF16)
        u_ref[rows, :] = _dot(h, w_ref[:, qkv_w:])


def _inproj(x2, g, w, l, attn_w, u_w):
    m, d = x2.shape
    kern = functools.partial(_inproj_kernel, attn_w=attn_w, chunk=512)
    return pl.pallas_call(
        kern,
        grid=(m // TOKEN_TILE,),
        in_specs=[
            pl.BlockSpec((TOKEN_TILE, d), lambda i: (i, 0)),
            _const_spec((None, 1, d), (l, 0, 0)),
            _const_spec((None, d, 3 * attn_w + u_w), (l, 0, 0)),
        ],
        out_specs=[
            pl.BlockSpec((TOKEN_TILE, 3 * attn_w), lambda i: (i, 0)),
            pl.BlockSpec((TOKEN_TILE, u_w), lambda i: (i, 0)),
        ],
        out_shape=[
            jax.ShapeDtypeStruct((m, 3 * attn_w), BF16),
            jax.ShapeDtypeStruct((m, u_w), F32),
        ],
        compiler_params=_params("arbitrary"),
        name="inproj",
    )(x2, g, w)


def _moba_kernel(q_ref, k_ref, v_ref, b0_ref, b1_ref, o_ref,
                 kmean_ref, vt_ref, *bufs, n_blocks):
    blk = MOBA_BLOCK
    for j in range(n_blocks):
        kb = k_ref[0, j * blk:(j + 1) * blk, :].astype(F32)
        kmean_ref[j:j + 1, :] = jnp.mean(kb, axis=0, keepdims=True)
        vt = v_ref[0, j * blk:(j + 1) * blk, :].astype(F32).T
        for h in range(HEADS_PER_STEP):
            vt_ref[h, 0:HEAD_DIM, j * blk:(j + 1) * blk] = (
                vt[h * HEAD_DIM:(h + 1) * HEAD_DIM].astype(BF16))
    for h in range(HEADS_PER_STEP):
        vt_ref[h, HEAD_DIM:HEAD_DIM + ONES_ROWS, :] = jnp.ones((ONES_ROWS, vt_ref.shape[2]), BF16)

    dim_row = lax.broadcasted_iota(jnp.int32, (LANES, blk), 0)
    blk_row = lax.broadcasted_iota(jnp.int32, (n_blocks, blk), 0)
    kmean = kmean_ref[...]

    n_buf = len(bufs) // 2
    s_refs, p_refs = bufs[:n_buf], bufs[n_buf:]
    qts = {}

    def pass1(qi, h, s_ref):
        if qi not in qts:
            qts[qi] = q_ref[0, qi * blk:(qi + 1) * blk, :].astype(F32).T
        in_head = (dim_row >= h * HEAD_DIM) & (dim_row < (h + 1) * HEAD_DIM)
        qt_h = jnp.where(in_head, qts[qi], 0.0)
        qt_hb = (qt_h * LOG2E).astype(BF16)

        gate = jnp.dot(kmean, qt_h, preferred_element_type=F32,
                       precision=lax.Precision.HIGHEST)
        rank = jnp.zeros((n_blocks, blk), jnp.int32)
        for j in range(qi):
            other = gate[j:j + 1, :]
            beats = (other > gate) | ((other == gate) & (j < blk_row))
            rank = rank + jnp.where(beats, 1, 0)
        sel = (blk_row < qi) & (rank < MOBA_TOPK)

        def store_scores(r, lo):
            rb = r.astype(BF16)
            s_ref[lo:lo + blk, :] = rb
            slab_max = rb[0:PACK]
            for i in range(PACK, blk, PACK):
                slab_max = jnp.maximum(slab_max, rb[i:i + PACK])
            return jnp.max(slab_max.astype(F32), axis=0, keepdims=True)

        col_max = []
        n_far = max(qi - 1, 0)
        if n_far:
            r = _dot(k_ref[0, 0:n_far * blk, :], qt_hb)
            for j in range(n_far):
                col_max.append(store_scores(r[j * blk:(j + 1) * blk], j * blk))
        if qi >= 1:
            lo = (qi - 1) * blk
            col_max.append(store_scores(_dot(k_ref[0, lo:lo + blk, :], qt_hb) + b1_ref[h], lo))
        lo = qi * blk
        m = store_scores(_dot(k_ref[0, lo:lo + blk, :], qt_hb) + b0_ref[h], lo)
        chosen = [sel[j:j + 1, :] for j in range(qi)]
        for j in range(qi):
            m = jnp.maximum(m, jnp.where(chosen[j], col_max[j], NEG))
        offs = [jnp.where(chosen[j], m, -NEG) for j in range(qi)] + [m]
        return [jnp.broadcast_to(off, (PACK, blk)).astype(BF16) for off in offs]

    def pass2(qi, h, offs, s_ref, p_ref):
        for j in range(qi + 1):
            for i in range(j * blk, (j + 1) * blk, PACK):
                p_ref[i:i + PACK, :] = jnp.exp2(s_ref[i:i + PACK, :] - offs[j])
        n_keys = (qi + 1) * blk
        acc = _dot(vt_ref[h, :, 0:n_keys], p_ref[0:n_keys, :])
        return acc[0:HEAD_DIM] / acc[HEAD_DIM:HEAD_DIM + 1]

    units = [(qi, h) for qi in range(n_blocks) for h in range(HEADS_PER_STEP)]
    offs, outs = {}, {}
    for n in range(len(units) + MOBA_AHEAD):
        if n < len(units):
            offs[n] = pass1(*units[n], s_refs[n % n_buf])
        if n >= MOBA_AHEAD:
            d = n - MOBA_AHEAD
            qi, h = units[d]
            outs[h] = pass2(qi, h, offs.pop(d), s_refs[d % n_buf], p_refs[d % n_buf])
            if h == HEADS_PER_STEP - 1:
                both = jnp.concatenate([outs[i] for i in range(HEADS_PER_STEP)], axis=0)
                o_ref[0, qi * blk:(qi + 1) * blk, :] = both.T.astype(o_ref.dtype)


def _moba(qkv3, bias0_t, bias1_t, attn_w):
    b, s, _ = qkv3.shape
    n_heads = attn_w // HEAD_DIM
    n_blocks = s // MOBA_BLOCK
    hp = n_heads // HEADS_PER_STEP
    cols = attn_w // LANES
    n_buf = MOBA_BUFFERS
    kern = functools.partial(_moba_kernel, n_blocks=n_blocks)
    bias_spec = pl.BlockSpec((HEADS_PER_STEP, MOBA_BLOCK, MOBA_BLOCK), lambda p, i: (p, 0, 0))
    return pl.pallas_call(
        kern,
        grid=(hp, b),
        in_specs=[
            pl.BlockSpec((1, s, LANES), lambda p, i: (i, 0, p)),
            pl.BlockSpec((1, s, LANES), lambda p, i: (i, 0, cols + p)),
            pl.BlockSpec((1, s, LANES), lambda p, i: (i, 0, 2 * cols + p)),
            bias_spec,
            bias_spec,
        ],
        out_specs=pl.BlockSpec((1, s, LANES), lambda p, i: (i, 0, p)),
        out_shape=jax.ShapeDtypeStruct((b, s, attn_w), BF16),
        scratch_shapes=[
            pltpu.VMEM((n_blocks, LANES), F32),
            pltpu.VMEM((HEADS_PER_STEP, HEAD_DIM + ONES_ROWS, s), BF16),
        ] + [pltpu.VMEM((s, MOBA_BLOCK), BF16)] * (2 * n_buf),
        compiler_params=_params("arbitrary", "arbitrary"),
        name="moba",
    )(qkv3, qkv3, qkv3, bias0_t, bias1_t)


def _t5_bucket(rel):
    n = jnp.maximum(rel, 0)
    max_exact = REL_BUCKETS // 2
    nf = jnp.maximum(n, 1).astype(F32)
    large = max_exact + (jnp.log(nf / max_exact) / math.log(REL_MAX_DIST / max_exact)
                         * (REL_BUCKETS - max_exact)).astype(jnp.int32)
    large = jnp.minimum(large, REL_BUCKETS - 1)
    return jnp.where(n < max_exact, n, large)


def _bias_tables(rel_bias):
    assert MOBA_BLOCK + 1 >= REL_MAX_DIST
    table_t = rel_bias.T.astype(F32)
    offs = jnp.arange(MOBA_BLOCK)
    rel_own = offs[None, :] - offs[:, None]
    buckets = jnp.arange(REL_BUCKETS)[:, None, None]

    def lookup(rel):
        onehot = (_t5_bucket(rel)[None] == buckets).astype(F32)
        return jnp.einsum('hb,bkq->hkq', table_t, onehot, precision=lax.Precision.HIGHEST)

    bias0 = jnp.where(rel_own >= 0, lookup(rel_own), NEG)
    bias1 = lookup(rel_own + MOBA_BLOCK)
    far = _t5_bucket(jnp.int32(2 * MOBA_BLOCK - (MOBA_BLOCK - 1)))
    bias_far = table_t[:, far][:, None, None]
    return (bias0 - bias_far) * LOG2E, (bias1 - bias_far) * LOG2E


def _pool_lane_tiles(pool_w):
    group_w = pool_w // len(POOL_WINDOWS)
    tiles = []
    for lo in range(0, pool_w, LANES):
        windows = [w for gi, w in enumerate(POOL_WINDOWS) if lo <= gi * group_w < lo + LANES]
        lane_group = (lax.broadcasted_iota(jnp.int32, (SEQ_TILE, LANES), 1) + lo) // group_w
        window = jnp.zeros((SEQ_TILE, LANES), F32)
        for gi, w in enumerate(POOL_WINDOWS):
            window = jnp.where(lane_group == gi, float(w), window)
        tiles.append((slice(lo, lo + LANES), windows, window))
    return tiles


def _merge_kernel(x_ref, attn_ref, u_ref, gpre_ref, wg_ref, bg_ref, wa_ref, wp_ref, wc_ref, wo_ref,
                  gpost_ref, wpg_ref, ps_ref, wdw_ref, bdw_ref, lng_ref, lnb_ref, o_ref,
                  prot_ref, crot_ref, *, tiles_per_seq, pool_w, conv_w):
    d = x_ref.shape[1]
    n_rows = x_ref.shape[0]
    step = pl.program_id(0)
    tile_in_seq = step % tiles_per_seq
    seq_start = tile_in_seq == 0
    lane_tiles = _pool_lane_tiles(pool_w)
    conv_lanes = slice(0, conv_w)
    copies = [(prot_ref, lanes, min(SUBLANES, max(windows))) for lanes, windows, _ in lane_tiles]
    copies.append((crot_ref, conv_lanes, SUBLANES))

    @pl.when(step == 0)
    def _first_halo_source():
        for ref, lanes, n_copies in copies:
            for r in range(n_copies):
                ref[r, n_rows:n_rows + SEQ_PAD, lanes] = jnp.zeros((SEQ_PAD, lanes.stop - lanes.start), F32)

    for ref, lanes, n_copies in copies:
        for r in range(n_copies):
            tail = ref[r, n_rows:n_rows + SEQ_PAD, lanes]
            ref[r, 0:SEQ_PAD, lanes] = jnp.where(seq_start, 0.0, tail)

    def delayed(ref, delay, t0, lanes):
        a8, r = delay // SUBLANES * SUBLANES, delay % SUBLANES
        return ref[r, pl.ds(SEQ_PAD + t0 - a8, SEQ_TILE), lanes]

    for t0 in range(0, n_rows, SEQ_TILE):
        rows = pl.ds(t0, SEQ_TILE)
        here = pl.ds(SEQ_PAD + t0, SEQ_TILE)
        prot_ref[0, here, 0:pool_w] = u_ref[rows, 0:pool_w]
        a = u_ref[rows, pool_w:pool_w + conv_w]
        gt = u_ref[rows, pool_w + conv_w:pool_w + 2 * conv_w]
        crot_ref[0, here, conv_lanes] = a * jax.nn.sigmoid(gt)
        for ref, lanes, n_copies in copies:
            for r in range(1, n_copies):
                ref[r, here, lanes] = ref[0, pl.ds(SEQ_PAD + t0 - r, SEQ_TILE), lanes]

    row = lax.broadcasted_iota(jnp.int32, (SEQ_TILE, LANES), 0)

    def pool_branch(t0):
        parts = []
        for lanes, windows, window in lane_tiles:
            tok = delayed(prot_ref, 0, t0, lanes)
            run = tok
            wsum = jnp.zeros_like(tok)
            for k in range(1, max(windows)):
                run = run + delayed(prot_ref, k, t0, lanes)
                if k + 1 in windows:
                    wsum = jnp.where(window == float(k + 1), run, wsum)
            seen = (row + (tile_in_seq * n_rows + t0 + 1)).astype(F32)
            parts.append(wsum / jnp.minimum(seen, window) - tok)
        y = _dot(jnp.concatenate(parts, axis=1).astype(BF16), wpg_ref[...]) * ps_ref[...]
        return y.astype(BF16)

    def conv_branch(t0):
        acc = jnp.zeros((SEQ_TILE, conv_w), F32) + bdw_ref[...]
        for k in range(CONV_KERNEL):
            acc = acc + delayed(crot_ref, CONV_KERNEL - 1 - k, t0, conv_lanes) * wdw_ref[k:k + 1, :]
        mu = jnp.mean(acc, axis=-1, keepdims=True)
        cen = acc - mu
        var = jnp.mean(cen * cen, axis=-1, keepdims=True)
        yn = cen * lax.rsqrt(var + EPS) * lng_ref[...] + lnb_ref[...]
        return (yn * jax.nn.sigmoid(yn)).astype(BF16)

    groups = [slice(t0, t0 + SEQ_TILE) for t0 in range(0, n_rows, SEQ_TILE)]
    xs = [x_ref[rows, :] for rows in groups]
    hs = [_rms(x, gpre_ref[...]).astype(BF16) for x in xs]
    ys = []
    for rows, h in zip(groups, hs):
        acts = (attn_ref[rows, :], pool_branch(rows.start), conv_branch(rows.start))
        merged = None
        for i, (act, w_ref) in enumerate(zip(acts, (wa_ref, wp_ref, wc_ref))):
            cols = slice(i * d, (i + 1) * d)
            gate = jax.nn.sigmoid(_dot(h, wg_ref[0, :, cols]) + bg_ref[:, cols])
            part = gate * _dot(act, w_ref[...])
            merged = part if merged is None else merged + part
        ys.append(_dot(merged.astype(BF16), wo_ref[...]))
    for rows, x, y in zip(groups, xs, ys):
        o_ref[rows, :] = x + _rms(y, gpost_ref[...])


def _merge(x2, attn, u, seq_len, gpre, w_in, bg, wa, wp, wc, wo, gpost, wpg, ps, wdw, bdw, lng, lnb, l):
    m, d = x2.shape
    gate_w = bg.shape[2]
    pool_w, conv_w = wp.shape[1], wc.shape[1]
    tile = lambda w: pl.BlockSpec((MERGE_TILE, w), lambda i: (i, 0))
    layer = lambda a: _const_spec((None,) + a.shape[1:], (l, 0, 0))
    gate_cols = pl.BlockSpec((pl.Element(1), pl.Element(d), pl.Element(gate_w)),
                             lambda i: (l, 0, w_in.shape[2] - gate_w), pipeline_mode=pl.Buffered(1))
    kern = functools.partial(_merge_kernel, tiles_per_seq=seq_len // MERGE_TILE,
                             pool_w=pool_w, conv_w=conv_w)
    return pl.pallas_call(
        kern,
        grid=(m // MERGE_TILE,),
        in_specs=[
            tile(d), tile(attn.shape[1]), tile(u.shape[1]),
            layer(gpre), gate_cols, layer(bg), layer(wa), layer(wp), layer(wc), layer(wo),
            layer(gpost), layer(wpg), layer(ps), layer(wdw), layer(bdw), layer(lng), layer(lnb),
        ],
        out_specs=tile(d),
        out_shape=jax.ShapeDtypeStruct((m, d), F32),
        scratch_shapes=[
            pltpu.VMEM((SUBLANES, SEQ_PAD + MERGE_TILE + SEQ_PAD, pool_w), F32),
            pltpu.VMEM((SUBLANES, SEQ_PAD + MERGE_TILE + SEQ_PAD, conv_w), F32),
        ],
        compiler_params=_params("arbitrary"),
        name="merge",
    )(x2, attn, u, gpre, w_in, bg, wa, wp, wc, wo, gpost, wpg, ps, wdw, bdw, lng, lnb)


def _ffn_kernel(x_ref, gpre_ref, win_ref, wout_ref, gpost_ref, o_ref, *, d_ff):
    groups = _row_groups(x_ref.shape[0], FFN_SUBTILES)
    xs = [x_ref[rows, :] for rows in groups]
    hs = [_rms(x, gpre_ref[...]).astype(BF16) for x in xs]
    fs = []
    for h in hs:
        f = None
        for c in range(0, d_ff, FFN_CHUNK):
            w = min(FFN_CHUNK, d_ff - c)
            gt = _dot(h, win_ref[:, c:c + w])
            up = _dot(h, win_ref[:, d_ff + c:d_ff + c + w])
            act = (gt * jax.nn.sigmoid(gt) * up).astype(BF16)
            part = _dot(act, wout_ref[c:c + w, :])
            f = part if f is None else f + part
        fs.append(f)
    for rows, x, f in zip(groups, xs, fs):
        o_ref[rows, :] = x + _rms(f, gpost_ref[...])


def _ffn(x2, gpre, win, wout, gpost, l):
    m, d = x2.shape
    d_ff = wout.shape[1]
    kern = functools.partial(_ffn_kernel, d_ff=d_ff)
    return pl.pallas_call(
        kern,
        grid=(m // TOKEN_TILE,),
        in_specs=[
            pl.BlockSpec((TOKEN_TILE, d), lambda i: (i, 0)),
            _const_spec((None, 1, d), (l, 0, 0)),
            _const_spec((None, d, 2 * d_ff), (l, 0, 0)),
            _const_spec((None, d_ff, d), (l, 0, 0)),
            _const_spec((None, 1, d), (l, 0, 0)),
        ],
        out_specs=pl.BlockSpec((TOKEN_TILE, d), lambda i: (i, 0)),
        out_shape=jax.ShapeDtypeStruct((m, d), F32),
        compiler_params=_params("arbitrary"),
        name="ffn",
    )(x2, gpre, win, wout, gpost)


def _block_diag(w):
    depth, groups, gd, _ = w.shape
    eye = jnp.eye(groups, dtype=w.dtype)
    return jnp.einsum('lgcd,gh->lgchd', w, eye).reshape(depth, groups * gd, groups * gd)


def kernel(x, pre_mix_g, w_in, b_gate, w_attn_o, rel_bias, w_pool_g, pool_scale, w_pool_o, w_dw,
           b_dw, conv_ln_g, conv_ln_b, w_conv_o, w_out, post_mix_g, pre_ffn_g, w_ffn_in,
           w_ffn_out, post_ffn_g):
    b, s, d = x.shape
    depth = w_in.shape[0]
    attn_w = w_attn_o.shape[1]
    pool_w = w_pool_o.shape[1]
    conv_w = w_conv_o.shape[1]
    u_w = pool_w + 2 * conv_w

    row = lambda a: a.reshape(depth, 1, a.shape[-1])
    w_in_b = w_in.astype(BF16)
    wa_b, wp_b, wc_b, wo_b = (w.astype(BF16) for w in (w_attn_o, w_pool_o, w_conv_o, w_out))
    wfi_b, wfo_b = w_ffn_in.astype(BF16), w_ffn_out.astype(BF16)
    wg_bd = _block_diag(w_pool_g).astype(BF16)
    bias0_t, bias1_t = _bias_tables(rel_bias)

    x2 = x.reshape(b * s, d)
    for l in range(depth):
        qkv, u = _inproj(x2, row(pre_mix_g), w_in_b, l, attn_w, u_w)
        attn = _moba(qkv.reshape(b, s, 3 * attn_w), bias0_t, bias1_t, attn_w)
        x2 = _merge(x2, attn.reshape(b * s, attn_w), u, s, row(pre_mix_g), w_in_b, row(b_gate),
                    wa_b, wp_b, wc_b, wo_b, row(post_mix_g), wg_bd, row(pool_scale), w_dw,
                    row(b_dw), row(conv_ln_g), row(conv_ln_b), l)
        x2 = _ffn(x2, row(pre_ffn_g), wfi_b, wfo_b, row(post_ffn_g), l)
    return x2.reshape(b, s, d)
```

```python
import functools
import math

import jax
import jax.numpy as jnp
from jax import lax
from jax.experimental import pallas as pl
from jax.experimental.pallas import tpu as pltpu

HEAD_DIM = 64
MOBA_BLOCK = 256
MOBA_TOPK = 3
POOL_WINDOWS = (2, 4, 8, 16)
CONV_KERNEL = 31
REL_BUCKETS = 32
REL_MAX_DIST = 128
EPS = 1e-6
NEG = -1e30
LOG2E = math.log2(math.e)

LANES = 128
HEADS_PER_STEP = LANES // HEAD_DIM
TOKEN_TILE = 1024
INPROJ_SUBTILES = 2
MERGE_TILE = 512
FFN_SUBTILES = 4
FFN_CHUNK = 1024
VMEM_LIMIT = 56 * 1024 * 1024
SUBLANES = 8
SEQ_PAD = 32
SEQ_TILE = 256
ONES_ROWS = 16
MOBA_AHEAD = 1
MOBA_BUFFERS = 4

F32 = jnp.float32
BF16 = jnp.bfloat16


def _params(*sem):
    return pltpu.CompilerParams(dimension_semantics=sem, vmem_limit_bytes=VMEM_LIMIT)


def _const_spec(shape, index):
    return pl.BlockSpec(shape, lambda *_: index, pipeline_mode=pl.Buffered(1))


def _rms(x, g):
    return x * lax.rsqrt(jnp.mean(x * x, axis=-1, keepdims=True) + EPS) * g


def _dot(a, b):
    return jnp.dot(a, b, preferred_element_type=F32)


def _row_groups(n_rows, n_groups):
    size = n_rows // n_groups
    return [slice(i * size, (i + 1) * size) for i in range(n_groups)]


def _inproj_kernel(x_ref, g_ref, w_ref, qkv_ref, u_ref, *, attn_w, chunk):
    qkv_w = 3 * attn_w
    groups = _row_groups(x_ref.shape[0], INPROJ_SUBTILES)
    hs = [_rms(x_ref[rows, :], g_ref[...]).astype(BF16) for rows in groups]
    for rows, h in zip(groups, hs):
        for c in range(0, qkv_w, chunk):
            r = _dot(h, w_ref[:, c:c + chunk])
            if c < attn_w:
                r = r * (HEAD_DIM ** -0.5)
            qkv_ref[rows, c:c + chunk] = r.astype(BF16)
        u_ref[rows, :] = _dot(h, w_ref[:, qkv_w:])


def _inproj(x2, g, w, l, attn_w, u_w):
    m, d = x2.shape
    kern = functools.partial(_inproj_kernel, attn_w=attn_w, chunk=512)
    return pl.pallas_call(
        kern,
        grid=(m // TOKEN_TILE,),
        in_specs=[
            pl.BlockSpec((TOKEN_TILE, d), lambda i: (i, 0)),
            _const_spec((None, 1, d), (l, 0, 0)),
            _const_spec((None, d, 3 * attn_w + u_w), (l, 0, 0)),
        ],
        out_specs=[
            pl.BlockSpec((TOKEN_TILE, 3 * attn_w), lambda i: (i, 0)),
            pl.BlockSpec((TOKEN_TILE, u_w), lambda i: (i, 0)),
        ],
        out_shape=[
            jax.ShapeDtypeStruct((m, 3 * attn_w), BF16),
            jax.ShapeDtypeStruct((m, u_w), F32),
        ],
        compiler_params=_params("arbitrary"),
        name="inproj",
    )(x2, g, w)


def _moba_kernel(q_ref, k_ref, v_ref, b0_ref, b1_ref, o_ref,
                 kmean_ref, vt_ref, *bufs, n_blocks):
    blk = MOBA_BLOCK
    for j in range(n_blocks):
        kb = k_ref[0, j * blk:(j + 1) * blk, :].astype(F32)
        kmean_ref[j:j + 1, :] = jnp.mean(kb, axis=0, keepdims=True)
        vt = v_ref[0, j * blk:(j + 1) * blk, :].astype(F32).T
        for h in range(HEADS_PER_STEP):
            vt_ref[h, 0:HEAD_DIM, j * blk:(j + 1) * blk] = (
                vt[h * HEAD_DIM:(h + 1) * HEAD_DIM].astype(BF16))
    for h in range(HEADS_PER_STEP):
        vt_ref[h, HEAD_DIM:HEAD_DIM + ONES_ROWS, :] = jnp.ones((ONES_ROWS, vt_ref.shape[2]), BF16)

    dim_row = lax.broadcasted_iota(jnp.int32, (LANES, blk), 0)
    blk_row = lax.broadcasted_iota(jnp.int32, (n_blocks, blk), 0)
    kmean = kmean_ref[...]

    n_buf = len(bufs) // 2
    s_refs, p_refs = bufs[:n_buf], bufs[n_buf:]
    qts = {}

    def pass1(qi, h, s_ref):
        if qi not in qts:
            qts[qi] = q_ref[0, qi * blk:(qi + 1) * blk, :].astype(F32).T
        in_head = (dim_row >= h * HEAD_DIM) & (dim_row < (h + 1) * HEAD_DIM)
        qt_h = jnp.where(in_head, qts[qi], 0.0)
        qt_hb = (qt_h * LOG2E).astype(BF16)

        gate = jnp.dot(kmean, qt_h, preferred_element_type=F32,
                       precision=lax.Precision.HIGHEST)
        rank = jnp.zeros((n_blocks, blk), jnp.int32)
        for j in range(qi):
            other = gate[j:j + 1, :]
            beats = (other > gate) | ((other == gate) & (j < blk_row))
            rank = rank + jnp.where(beats, 1, 0)
        sel = (blk_row < qi) & (rank < MOBA_TOPK)

        col_max = []
        n_far = max(qi - 1, 0)
        if n_far:
            r = _dot(k_ref[0, 0:n_far * blk, :], qt_hb)
            s_ref[0:n_far * blk, :] = r
            for j in range(n_far):
                col_max.append(jnp.max(r[j * blk:(j + 1) * blk], axis=0, keepdims=True))
        if qi >= 1:
            lo = (qi - 1) * blk
            r = _dot(k_ref[0, lo:lo + blk, :], qt_hb) + b1_ref[h]
            s_ref[lo:lo + blk, :] = r
            col_max.append(jnp.max(r, axis=0, keepdims=True))
        lo = qi * blk
        r = _dot(k_ref[0, lo:lo + blk, :], qt_hb) + b0_ref[h]
        s_ref[lo:lo + blk, :] = r
        m = jnp.max(r, axis=0, keepdims=True)
        chosen = [sel[j:j + 1, :] for j in range(qi)]
        for j in range(qi):
            m = jnp.maximum(m, jnp.where(chosen[j], col_max[j], NEG))
        return [jnp.where(chosen[j], m, -NEG) for j in range(qi)] + [m]

    def pass2(qi, h, offs, s_ref, p_ref):
        for j in range(qi + 1):
            x = s_ref[j * blk:(j + 1) * blk, :] - offs[j]
            p_ref[j * blk:(j + 1) * blk, :] = jnp.exp2(x.astype(BF16))
        n_keys = (qi + 1) * blk
        acc = _dot(vt_ref[h, :, 0:n_keys], p_ref[0:n_keys, :])
        return acc[0:HEAD_DIM] / acc[HEAD_DIM:HEAD_DIM + 1]

    units = [(qi, h) for qi in range(n_blocks) for h in range(HEADS_PER_STEP)]
    offs, outs = {}, {}
    for n in range(len(units) + MOBA_AHEAD):
        if n < len(units):
            offs[n] = pass1(*units[n], s_refs[n % n_buf])
        if n >= MOBA_AHEAD:
            d = n - MOBA_AHEAD
            qi, h = units[d]
            outs[h] = pass2(qi, h, offs.pop(d), s_refs[d % n_buf], p_refs[d % n_buf])
            if h == HEADS_PER_STEP - 1:
                both = jnp.concatenate([outs[i] for i in range(HEADS_PER_STEP)], axis=0)
                o_ref[0, qi * blk:(qi + 1) * blk, :] = both.T.astype(o_ref.dtype)


def _moba(qkv3, bias0_t, bias1_t, attn_w):
    b, s, _ = qkv3.shape
    n_heads = attn_w // HEAD_DIM
    n_blocks = s // MOBA_BLOCK
    hp = n_heads // HEADS_PER_STEP
    cols = attn_w // LANES
    n_buf = MOBA_BUFFERS
    kern = functools.partial(_moba_kernel, n_blocks=n_blocks)
    bias_spec = pl.BlockSpec((HEADS_PER_STEP, MOBA_BLOCK, MOBA_BLOCK), lambda p, i: (p, 0, 0))
    return pl.pallas_call(
        kern,
        grid=(hp, b),
        in_specs=[
            pl.BlockSpec((1, s, LANES), lambda p, i: (i, 0, p)),
            pl.BlockSpec((1, s, LANES), lambda p, i: (i, 0, cols + p)),
            pl.BlockSpec((1, s, LANES), lambda p, i: (i, 0, 2 * cols + p)),
            bias_spec,
            bias_spec,
        ],
        out_specs=pl.BlockSpec((1, s, LANES), lambda p, i: (i, 0, p)),
        out_shape=jax.ShapeDtypeStruct((b, s, attn_w), BF16),
        scratch_shapes=[
            pltpu.VMEM((n_blocks, LANES), F32),
            pltpu.VMEM((HEADS_PER_STEP, HEAD_DIM + ONES_ROWS, s), BF16),
        ] + [pltpu.VMEM((s, MOBA_BLOCK), F32)] * n_buf + [pltpu.VMEM((s, MOBA_BLOCK), BF16)] * n_buf,
        compiler_params=_params("arbitrary", "arbitrary"),
        name="moba",
    )(qkv3, qkv3, qkv3, bias0_t, bias1_t)


def _t5_bucket(rel):
    n = jnp.maximum(rel, 0)
    max_exact = REL_BUCKETS // 2
    nf = jnp.maximum(n, 1).astype(F32)
    large = max_exact + (jnp.log(nf / max_exact) / math.log(REL_MAX_DIST / max_exact)
                         * (REL_BUCKETS - max_exact)).astype(jnp.int32)
    large = jnp.minimum(large, REL_BUCKETS - 1)
    return jnp.where(n < max_exact, n, large)


def _bias_tables(rel_bias):
    assert MOBA_BLOCK + 1 >= REL_MAX_DIST
    table_t = rel_bias.T.astype(F32)
    offs = jnp.arange(MOBA_BLOCK)
    rel_own = offs[None, :] - offs[:, None]
    buckets = jnp.arange(REL_BUCKETS)[:, None, None]

    def lookup(rel):
        onehot = (_t5_bucket(rel)[None] == buckets).astype(F32)
        return jnp.einsum('hb,bkq->hkq', table_t, onehot, precision=lax.Precision.HIGHEST)

    bias0 = jnp.where(rel_own >= 0, lookup(rel_own), NEG)
    bias1 = lookup(rel_own + MOBA_BLOCK)
    far = _t5_bucket(jnp.int32(2 * MOBA_BLOCK - (MOBA_BLOCK - 1)))
    bias_far = table_t[:, far][:, None, None]
    return (bias0 - bias_far) * LOG2E, (bias1 - bias_far) * LOG2E


def _pool_lane_tiles(pool_w):
    group_w = pool_w // len(POOL_WINDOWS)
    tiles = []
    for lo in range(0, pool_w, LANES):
        windows = [w for gi, w in enumerate(POOL_WINDOWS) if lo <= gi * group_w < lo + LANES]
        lane_group = (lax.broadcasted_iota(jnp.int32, (SEQ_TILE, LANES), 1) + lo) // group_w
        window = jnp.zeros((SEQ_TILE, LANES), F32)
        for gi, w in enumerate(POOL_WINDOWS):
            window = jnp.where(lane_group == gi, float(w), window)
        tiles.append((slice(lo, lo + LANES), windows, window))
    return tiles


def _merge_kernel(x_ref, attn_ref, u_ref, gpre_ref, wg_ref, bg_ref, wa_ref, wp_ref, wc_ref, wo_ref,
                  gpost_ref, wpg_ref, ps_ref, wdw_ref, bdw_ref, lng_ref, lnb_ref, o_ref,
                  prot_ref, crot_ref, *, tiles_per_seq, pool_w, conv_w):
    d = x_ref.shape[1]
    n_rows = x_ref.shape[0]
    step = pl.program_id(0)
    tile_in_seq = step % tiles_per_seq
    seq_start = tile_in_seq == 0
    lane_tiles = _pool_lane_tiles(pool_w)
    conv_lanes = slice(0, conv_w)
    copies = [(prot_ref, lanes, min(SUBLANES, max(windows))) for lanes, windows, _ in lane_tiles]
    copies.append((crot_ref, conv_lanes, SUBLANES))

    @pl.when(step == 0)
    def _first_halo_source():
        for ref, lanes, n_copies in copies:
            for r in range(n_copies):
                ref[r, n_rows:n_rows + SEQ_PAD, lanes] = jnp.zeros((SEQ_PAD, lanes.stop - lanes.start), F32)

    for ref, lanes, n_copies in copies:
        for r in range(n_copies):
            tail = ref[r, n_rows:n_rows + SEQ_PAD, lanes]
            ref[r, 0:SEQ_PAD, lanes] = jnp.where(seq_start, 0.0, tail)

    def delayed(ref, delay, t0, lanes):
        a8, r = delay // SUBLANES * SUBLANES, delay % SUBLANES
        return ref[r, pl.ds(SEQ_PAD + t0 - a8, SEQ_TILE), lanes]

    for t0 in range(0, n_rows, SEQ_TILE):
        rows = pl.ds(t0, SEQ_TILE)
        here = pl.ds(SEQ_PAD + t0, SEQ_TILE)
        prot_ref[0, here, 0:pool_w] = u_ref[rows, 0:pool_w]
        a = u_ref[rows, pool_w:pool_w + conv_w]
        gt = u_ref[rows, pool_w + conv_w:pool_w + 2 * conv_w]
        crot_ref[0, here, conv_lanes] = a * jax.nn.sigmoid(gt)
        for ref, lanes, n_copies in copies:
            for r in range(1, n_copies):
                ref[r, here, lanes] = ref[0, pl.ds(SEQ_PAD + t0 - r, SEQ_TILE), lanes]

    row = lax.broadcasted_iota(jnp.int32, (SEQ_TILE, LANES), 0)

    def pool_branch(t0):
        parts = []
        for lanes, windows, window in lane_tiles:
            tok = delayed(prot_ref, 0, t0, lanes)
            run = tok
            wsum = jnp.zeros_like(tok)
            for k in range(1, max(windows)):
                run = run + delayed(prot_ref, k, t0, lanes)
                if k + 1 in windows:
                    wsum = jnp.where(window == float(k + 1), run, wsum)
            seen = (row + (tile_in_seq * n_rows + t0 + 1)).astype(F32)
            parts.append(wsum / jnp.minimum(seen, window) - tok)
        y = _dot(jnp.concatenate(parts, axis=1).astype(BF16), wpg_ref[...]) * ps_ref[...]
        return y.astype(BF16)

    def conv_branch(t0):
        acc = jnp.zeros((SEQ_TILE, conv_w), F32) + bdw_ref[...]
        for k in range(CONV_KERNEL):
            acc = acc + delayed(crot_ref, CONV_KERNEL - 1 - k, t0, conv_lanes) * wdw_ref[k:k + 1, :]
        mu = jnp.mean(acc, axis=-1, keepdims=True)
        cen = acc - mu
        var = jnp.mean(cen * cen, axis=-1, keepdims=True)
        yn = cen * lax.rsqrt(var + EPS) * lng_ref[...] + lnb_ref[...]
        return (yn * jax.nn.sigmoid(yn)).astype(BF16)

    groups = [slice(t0, t0 + SEQ_TILE) for t0 in range(0, n_rows, SEQ_TILE)]
    xs = [x_ref[rows, :] for rows in groups]
    hs = [_rms(x, gpre_ref[...]).astype(BF16) for x in xs]
    ys = []
    for rows, h in zip(groups, hs):
        acts = (attn_ref[rows, :], pool_branch(rows.start), conv_branch(rows.start))
        merged = None
        for i, (act, w_ref) in enumerate(zip(acts, (wa_ref, wp_ref, wc_ref))):
            cols = slice(i * d, (i + 1) * d)
            gate = jax.nn.sigmoid(_dot(h, wg_ref[0, :, cols]) + bg_ref[:, cols])
            part = gate * _dot(act, w_ref[...])
            merged = part if merged is None else merged + part
        ys.append(_dot(merged.astype(BF16), wo_ref[...]))
    for rows, x, y in zip(groups, xs, ys):
        o_ref[rows, :] = x + _rms(y, gpost_ref[...])


def _merge(x2, attn, u, seq_len, gpre, w_in, bg, wa, wp, wc, wo, gpost, wpg, ps, wdw, bdw, lng, lnb, l):
    m, d = x2.shape
    gate_w = bg.shape[2]
    pool_w, conv_w = wp.shape[1], wc.shape[1]
    tile = lambda w: pl.BlockSpec((MERGE_TILE, w), lambda i: (i, 0))
    layer = lambda a: _const_spec((None,) + a.shape[1:], (l, 0, 0))
    gate_cols = pl.BlockSpec((pl.Element(1), pl.Element(d), pl.Element(gate_w)),
                             lambda i: (l, 0, w_in.shape[2] - gate_w), pipeline_mode=pl.Buffered(1))
    kern = functools.partial(_merge_kernel, tiles_per_seq=seq_len // MERGE_TILE,
                             pool_w=pool_w, conv_w=conv_w)
    return pl.pallas_call(
        kern,
        grid=(m // MERGE_TILE,),
        in_specs=[
            tile(d), tile(attn.shape[1]), tile(u.shape[1]),
            layer(gpre), gate_cols, layer(bg), layer(wa), layer(wp), layer(wc), layer(wo),
            layer(gpost), layer(wpg), layer(ps), layer(wdw), layer(bdw), layer(lng), layer(lnb),
        ],
        out_specs=tile(d),
        out_shape=jax.ShapeDtypeStruct((m, d), F32),
        scratch_shapes=[
            pltpu.VMEM((SUBLANES, SEQ_PAD + MERGE_TILE + SEQ_PAD, pool_w), F32),
            pltpu.VMEM((SUBLANES, SEQ_PAD + MERGE_TILE + SEQ_PAD, conv_w), F32),
        ],
        compiler_params=_params("arbitrary"),
        name="merge",
    )(x2, attn, u, gpre, w_in, bg, wa, wp, wc, wo, gpost, wpg, ps, wdw, bdw, lng, lnb)


def _ffn_kernel(x_ref, gpre_ref, win_ref, wout_ref, gpost_ref, o_ref, *, d_ff):
    groups = _row_groups(x_ref.shape[0], FFN_SUBTILES)
    xs = [x_ref[rows, :] for rows in groups]
    hs = [_rms(x, gpre_ref[...]).astype(BF16) for x in xs]
    fs = []
    for h in hs:
        f = None
        for c in range(0, d_ff, FFN_CHUNK):
            w = min(FFN_CHUNK, d_ff - c)
            gt = _dot(h, win_ref[:, c:c + w])
            up = _dot(h, win_ref[:, d_ff + c:d_ff + c + w])
            act = (gt * jax.nn.sigmoid(gt) * up).astype(BF16)
            part = _dot(act, wout_ref[c:c + w, :])
            f = part if f is None else f + part
        fs.append(f)
    for rows, x, f in zip(groups, xs, fs):
        o_ref[rows, :] = x + _rms(f, gpost_ref[...])


def _ffn(x2, gpre, win, wout, gpost, l):
    m, d = x2.shape
    d_ff = wout.shape[1]
    kern = functools.partial(_ffn_kernel, d_ff=d_ff)
    return pl.pallas_call(
        kern,
        grid=(m // TOKEN_TILE,),
        in_specs=[
            pl.BlockSpec((TOKEN_TILE, d), lambda i: (i, 0)),
            _const_spec((None, 1, d), (l, 0, 0)),
            _const_spec((None, d, 2 * d_ff), (l, 0, 0)),
            _const_spec((None, d_ff, d), (l, 0, 0)),
            _const_spec((None, 1, d), (l, 0, 0)),
        ],
        out_specs=pl.BlockSpec((TOKEN_TILE, d), lambda i: (i, 0)),
        out_shape=jax.ShapeDtypeStruct((m, d), F32),
        compiler_params=_params("arbitrary"),
        name="ffn",
    )(x2, gpre, win, wout, gpost)


def _block_diag(w):
    depth, groups, gd, _ = w.shape
    eye = jnp.eye(groups, dtype=w.dtype)
    return jnp.einsum('lgcd,gh->lgchd', w, eye).reshape(depth, groups * gd, groups * gd)


def kernel(x, pre_mix_g, w_in, b_gate, w_attn_o, rel_bias, w_pool_g, pool_scale, w_pool_o, w_dw,
           b_dw, conv_ln_g, conv_ln_b, w_conv_o, w_out, post_mix_g, pre_ffn_g, w_ffn_in,
           w_ffn_out, post_ffn_g):
    b, s, d = x.shape
    depth = w_in.shape[0]
    attn_w = w_attn_o.shape[1]
    pool_w = w_pool_o.shape[1]
    conv_w = w_conv_o.shape[1]
    u_w = pool_w + 2 * conv_w

    row = lambda a: a.reshape(depth, 1, a.shape[-1])
    w_in_b = w_in.astype(BF16)
    wa_b, wp_b, wc_b, wo_b = (w.astype(BF16) for w in (w_attn_o, w_pool_o, w_conv_o, w_out))
    wfi_b, wfo_b = w_ffn_in.astype(BF16), w_ffn_out.astype(BF16)
    wg_bd = _block_diag(w_pool_g).astype(BF16)
    bias0_t, bias1_t = _bias_tables(rel_bias)

    x2 = x.reshape(b * s, d)
    for l in range(depth):
        qkv, u = _inproj(x2, row(pre_mix_g), w_in_b, l, attn_w, u_w)
        attn = _moba(qkv.reshape(b, s, 3 * attn_w), bias0_t, bias1_t, attn_w)
        x2 = _merge(x2, attn.reshape(b * s, attn_w), u, s, row(pre_mix_g), w_in_b, row(b_gate),
                    wa_b, wp_b, wc_b, wo_b, row(post_mix_g), wg_bd, row(pool_scale), w_dw,
                    row(b_dw), row(conv_ln_g), row(conv_ln_b), l)
        x2 = _ffn(x2, row(pre_ffn_g), wfi_b, wfo_b, row(post_ffn_g), l)
    return x2.reshape(b, s, d)
```

```python
import functools
import math

import jax
import jax.numpy as jnp
from jax import lax
from jax.experimental import pallas as pl
from jax.experimental.pallas import tpu as pltpu

HEAD_DIM = 64
MOBA_BLOCK = 256
MOBA_TOPK = 3
POOL_WINDOWS = (2, 4, 8, 16)
CONV_KERNEL = 31
REL_BUCKETS = 32
REL_MAX_DIST = 128
EPS = 1e-6
NEG = -1e30
LOG2E = math.log2(math.e)

LANES = 128
HEADS_PER_STEP = LANES // HEAD_DIM
TOKEN_TILE = 1024
INPROJ_SUBTILES = 2
MERGE_TILE = 512
FFN_SUBTILES = 4
FFN_CHUNK = 1024
VMEM_LIMIT = 56 * 1024 * 1024
SUBLANES = 8
SEQ_PAD = 32
SEQ_TILE = 256
ONES_ROWS = 16
MOBA_AHEAD = 3
MOBA_BUFFERS = 8

F32 = jnp.float32
BF16 = jnp.bfloat16


def _params(*sem):
    return pltpu.CompilerParams(dimension_semantics=sem, vmem_limit_bytes=VMEM_LIMIT)


def _const_spec(shape, index):
    return pl.BlockSpec(shape, lambda *_: index, pipeline_mode=pl.Buffered(1))


def _rms(x, g):
    return x * lax.rsqrt(jnp.mean(x * x, axis=-1, keepdims=True) + EPS) * g


def _dot(a, b):
    return jnp.dot(a, b, preferred_element_type=F32)


def _row_groups(n_rows, n_groups):
    size = n_rows // n_groups
    return [slice(i * size, (i + 1) * size) for i in range(n_groups)]


def _inproj_kernel(x_ref, g_ref, w_ref, qkv_ref, u_ref, *, attn_w, chunk):
    qkv_w = 3 * attn_w
    groups = _row_groups(x_ref.shape[0], INPROJ_SUBTILES)
    hs = [_rms(x_ref[rows, :], g_ref[...]).astype(BF16) for rows in groups]
    for rows, h in zip(groups, hs):
        for c in range(0, qkv_w, chunk):
            r = _dot(h, w_ref[:, c:c + chunk])
            if c < attn_w:
                r = r * (HEAD_DIM ** -0.5)
            qkv_ref[rows, c:c + chunk] = r.astype(BF16)
        u_ref[rows, :] = _dot(h, w_ref[:, qkv_w:])


def _inproj(x2, g, w, l, attn_w, u_w):
    m, d = x2.shape
    kern = functools.partial(_inproj_kernel, attn_w=attn_w, chunk=512)
    return pl.pallas_call(
        kern,
        grid=(m // TOKEN_TILE,),
        in_specs=[
            pl.BlockSpec((TOKEN_TILE, d), lambda i: (i, 0)),
            _const_spec((None, 1, d), (l, 0, 0)),
            _const_spec((None, d, 3 * attn_w + u_w), (l, 0, 0)),
        ],
        out_specs=[
            pl.BlockSpec((TOKEN_TILE, 3 * attn_w), lambda i: (i, 0)),
            pl.BlockSpec((TOKEN_TILE, u_w), lambda i: (i, 0)),
        ],
        out_shape=[
            jax.ShapeDtypeStruct((m, 3 * attn_w), BF16),
            jax.ShapeDtypeStruct((m, u_w), F32),
        ],
        compiler_params=_params("arbitrary"),
        name="inproj",
    )(x2, g, w)


def _moba_kernel(q_ref, k_ref, v_ref, b0_ref, b1_ref, o_ref,
                 kmean_ref, vt_ref, *bufs, n_blocks):
    blk = MOBA_BLOCK
    for j in range(n_blocks):
        kb = k_ref[0, j * blk:(j + 1) * blk, :].astype(F32)
        kmean_ref[j:j + 1, :] = jnp.mean(kb, axis=0, keepdims=True)
        vt = v_ref[0, j * blk:(j + 1) * blk, :].astype(F32).T
        for h in range(HEADS_PER_STEP):
            vt_ref[h, 0:HEAD_DIM, j * blk:(j + 1) * blk] = (
                vt[h * HEAD_DIM:(h + 1) * HEAD_DIM].astype(BF16))
    for h in range(HEADS_PER_STEP):
        vt_ref[h, HEAD_DIM:HEAD_DIM + ONES_ROWS, :] = jnp.ones((ONES_ROWS, vt_ref.shape[2]), BF16)

    dim_row = lax.broadcasted_iota(jnp.int32, (LANES, blk), 0)
    blk_row = lax.broadcasted_iota(jnp.int32, (n_blocks, blk), 0)
    kmean = kmean_ref[...]

    n_buf = len(bufs) // 2
    s_refs, p_refs = bufs[:n_buf], bufs[n_buf:]
    qts = {}

    def pass1(qi, h, s_ref):
        if qi not in qts:
            qts[qi] = q_ref[0, qi * blk:(qi + 1) * blk, :].astype(F32).T
        in_head = (dim_row >= h * HEAD_DIM) & (dim_row < (h + 1) * HEAD_DIM)
        qt_h = jnp.where(in_head, qts[qi], 0.0)
        qt_hb = (qt_h * LOG2E).astype(BF16)

        gate = jnp.dot(kmean, qt_h, preferred_element_type=F32,
                       precision=lax.Precision.HIGHEST)
        rank = jnp.zeros((n_blocks, blk), jnp.int32)
        for j in range(qi):
            other = gate[j:j + 1, :]
            beats = (other > gate) | ((other == gate) & (j < blk_row))
            rank = rank + jnp.where(beats, 1, 0)
        sel = (blk_row < qi) & (rank < MOBA_TOPK)

        col_max = []
        n_far = max(qi - 1, 0)
        if n_far:
            r = _dot(k_ref[0, 0:n_far * blk, :], qt_hb)
            s_ref[0:n_far * blk, :] = r
            for j in range(n_far):
                col_max.append(jnp.max(r[j * blk:(j + 1) * blk], axis=0, keepdims=True))
        if qi >= 1:
            lo = (qi - 1) * blk
            r = _dot(k_ref[0, lo:lo + blk, :], qt_hb) + b1_ref[h]
            s_ref[lo:lo + blk, :] = r
            col_max.append(jnp.max(r, axis=0, keepdims=True))
        lo = qi * blk
        r = _dot(k_ref[0, lo:lo + blk, :], qt_hb) + b0_ref[h]
        s_ref[lo:lo + blk, :] = r
        m = jnp.max(r, axis=0, keepdims=True)
        chosen = [sel[j:j + 1, :] for j in range(qi)]
        for j in range(qi):
            m = jnp.maximum(m, jnp.where(chosen[j], col_max[j], NEG))
        return [jnp.where(chosen[j], m, -NEG) for j in range(qi)] + [m]

    def pass2(qi, h, offs, s_ref, p_ref):
        for j in range(qi + 1):
            x = s_ref[j * blk:(j + 1) * blk, :] - offs[j]
            p_ref[j * blk:(j + 1) * blk, :] = jnp.exp2(x.astype(BF16))
        n_keys = (qi + 1) * blk
        acc = _dot(vt_ref[h, :, 0:n_keys], p_ref[0:n_keys, :])
        return acc[0:HEAD_DIM] / acc[HEAD_DIM:HEAD_DIM + 1]

    units = [(qi, h) for qi in range(n_blocks) for h in range(HEADS_PER_STEP)]
    offs, outs = {}, {}
    for n in range(len(units) + MOBA_AHEAD):
        if n < len(units):
            offs[n] = pass1(*units[n], s_refs[n % n_buf])
        if n >= MOBA_AHEAD:
            d = n - MOBA_AHEAD
            qi, h = units[d]
            outs[h] = pass2(qi, h, offs.pop(d), s_refs[d % n_buf], p_refs[d % n_buf])
            if h == HEADS_PER_STEP - 1:
                both = jnp.concatenate([outs[i] for i in range(HEADS_PER_STEP)], axis=0)
                o_ref[0, qi * blk:(qi + 1) * blk, :] = both.T.astype(o_ref.dtype)


def _moba(qkv3, bias0_t, bias1_t, attn_w):
    b, s, _ = qkv3.shape
    n_heads = attn_w // HEAD_DIM
    n_blocks = s // MOBA_BLOCK
    hp = n_heads // HEADS_PER_STEP
    cols = attn_w // LANES
    n_buf = MOBA_BUFFERS
    kern = functools.partial(_moba_kernel, n_blocks=n_blocks)
    bias_spec = pl.BlockSpec((HEADS_PER_STEP, MOBA_BLOCK, MOBA_BLOCK), lambda p, i: (p, 0, 0))
    return pl.pallas_call(
        kern,
        grid=(hp, b),
        in_specs=[
            pl.BlockSpec((1, s, LANES), lambda p, i: (i, 0, p)),
            pl.BlockSpec((1, s, LANES), lambda p, i: (i, 0, cols + p)),
            pl.BlockSpec((1, s, LANES), lambda p, i: (i, 0, 2 * cols + p)),
            bias_spec,
            bias_spec,
        ],
        out_specs=pl.BlockSpec((1, s, LANES), lambda p, i: (i, 0, p)),
        out_shape=jax.ShapeDtypeStruct((b, s, attn_w), BF16),
        scratch_shapes=[
            pltpu.VMEM((n_blocks, LANES), F32),
            pltpu.VMEM((HEADS_PER_STEP, HEAD_DIM + ONES_ROWS, s), BF16),
        ] + [pltpu.VMEM((s, MOBA_BLOCK), F32)] * n_buf + [pltpu.VMEM((s, MOBA_BLOCK), BF16)] * n_buf,
        compiler_params=_params("arbitrary", "arbitrary"),
        name="moba",
    )(qkv3, qkv3, qkv3, bias0_t, bias1_t)


def _t5_bucket(rel):
    n = jnp.maximum(rel, 0)
    max_exact = REL_BUCKETS // 2
    nf = jnp.maximum(n, 1).astype(F32)
    large = max_exact + (jnp.log(nf / max_exact) / math.log(REL_MAX_DIST / max_exact)
                         * (REL_BUCKETS - max_exact)).astype(jnp.int32)
    large = jnp.minimum(large, REL_BUCKETS - 1)
    return jnp.where(n < max_exact, n, large)


def _bias_tables(rel_bias):
    assert MOBA_BLOCK + 1 >= REL_MAX_DIST
    table_t = rel_bias.T.astype(F32)
    offs = jnp.arange(MOBA_BLOCK)
    rel_own = offs[None, :] - offs[:, None]
    buckets = jnp.arange(REL_BUCKETS)[:, None, None]

    def lookup(rel):
        onehot = (_t5_bucket(rel)[None] == buckets).astype(F32)
        return jnp.einsum('hb,bkq->hkq', table_t, onehot, precision=lax.Precision.HIGHEST)

    bias0 = jnp.where(rel_own >= 0, lookup(rel_own), NEG)
    bias1 = lookup(rel_own + MOBA_BLOCK)
    far = _t5_bucket(jnp.int32(2 * MOBA_BLOCK - (MOBA_BLOCK - 1)))
    bias_far = table_t[:, far][:, None, None]
    return (bias0 - bias_far) * LOG2E, (bias1 - bias_far) * LOG2E


def _pool_lane_tiles(pool_w):
    group_w = pool_w // len(POOL_WINDOWS)
    tiles = []
    for lo in range(0, pool_w, LANES):
        windows = [w for gi, w in enumerate(POOL_WINDOWS) if lo <= gi * group_w < lo + LANES]
        lane_group = (lax.broadcasted_iota(jnp.int32, (SEQ_TILE, LANES), 1) + lo) // group_w
        window = jnp.zeros((SEQ_TILE, LANES), F32)
        for gi, w in enumerate(POOL_WINDOWS):
            window = jnp.where(lane_group == gi, float(w), window)
        tiles.append((slice(lo, lo + LANES), windows, window))
    return tiles


def _merge_kernel(x_ref, attn_ref, u_ref, gpre_ref, wg_ref, bg_ref, wa_ref, wp_ref, wc_ref, wo_ref,
                  gpost_ref, wpg_ref, ps_ref, wdw_ref, bdw_ref, lng_ref, lnb_ref, o_ref,
                  prot_ref, crot_ref, *, tiles_per_seq, pool_w, conv_w):
    d = x_ref.shape[1]
    n_rows = x_ref.shape[0]
    step = pl.program_id(0)
    tile_in_seq = step % tiles_per_seq
    seq_start = tile_in_seq == 0
    lane_tiles = _pool_lane_tiles(pool_w)
    conv_lanes = slice(0, conv_w)
    copies = [(prot_ref, lanes, min(SUBLANES, max(windows))) for lanes, windows, _ in lane_tiles]
    copies.append((crot_ref, conv_lanes, SUBLANES))

    @pl.when(step == 0)
    def _first_halo_source():
        for ref, lanes, n_copies in copies:
            for r in range(n_copies):
                ref[r, n_rows:n_rows + SEQ_PAD, lanes] = jnp.zeros((SEQ_PAD, lanes.stop - lanes.start), F32)

    for ref, lanes, n_copies in copies:
        for r in range(n_copies):
            tail = ref[r, n_rows:n_rows + SEQ_PAD, lanes]
            ref[r, 0:SEQ_PAD, lanes] = jnp.where(seq_start, 0.0, tail)

    def delayed(ref, delay, t0, lanes):
        a8, r = delay // SUBLANES * SUBLANES, delay % SUBLANES
        return ref[r, pl.ds(SEQ_PAD + t0 - a8, SEQ_TILE), lanes]

    for t0 in range(0, n_rows, SEQ_TILE):
        rows = pl.ds(t0, SEQ_TILE)
        here = pl.ds(SEQ_PAD + t0, SEQ_TILE)
        prot_ref[0, here, 0:pool_w] = u_ref[rows, 0:pool_w]
        a = u_ref[rows, pool_w:pool_w + conv_w]
        gt = u_ref[rows, pool_w + conv_w:pool_w + 2 * conv_w]
        crot_ref[0, here, conv_lanes] = a * jax.nn.sigmoid(gt)
        for ref, lanes, n_copies in copies:
            for r in range(1, n_copies):
                ref[r, here, lanes] = ref[0, pl.ds(SEQ_PAD + t0 - r, SEQ_TILE), lanes]

    row = lax.broadcasted_iota(jnp.int32, (SEQ_TILE, LANES), 0)

    def pool_branch(t0):
        parts = []
        for lanes, windows, window in lane_tiles:
            tok = delayed(prot_ref, 0, t0, lanes)
            run = tok
            wsum = jnp.zeros_like(tok)
            for k in range(1, max(windows)):
                run = run + delayed(prot_ref, k, t0, lanes)
                if k + 1 in windows:
                    wsum = jnp.where(window == float(k + 1), run, wsum)
            seen = (row + (tile_in_seq * n_rows + t0 + 1)).astype(F32)
            parts.append(wsum / jnp.minimum(seen, window) - tok)
        y = _dot(jnp.concatenate(parts, axis=1).astype(BF16), wpg_ref[...]) * ps_ref[...]
        return y.astype(BF16)

    def conv_branch(t0):
        acc = jnp.zeros((SEQ_TILE, conv_w), F32) + bdw_ref[...]
        for k in range(CONV_KERNEL):
            acc = acc + delayed(crot_ref, CONV_KERNEL - 1 - k, t0, conv_lanes) * wdw_ref[k:k + 1, :]
        mu = jnp.mean(acc, axis=-1, keepdims=True)
        cen = acc - mu
        var = jnp.mean(cen * cen, axis=-1, keepdims=True)
        yn = cen * lax.rsqrt(var + EPS) * lng_ref[...] + lnb_ref[...]
        return (yn * jax.nn.sigmoid(yn)).astype(BF16)

    groups = [slice(t0, t0 + SEQ_TILE) for t0 in range(0, n_rows, SEQ_TILE)]
    xs = [x_ref[rows, :] for rows in groups]
    hs = [_rms(x, gpre_ref[...]).astype(BF16) for x in xs]
    ys = []
    for rows, h in zip(groups, hs):
        acts = (attn_ref[rows, :], pool_branch(rows.start), conv_branch(rows.start))
        merged = None
        for i, (act, w_ref) in enumerate(zip(acts, (wa_ref, wp_ref, wc_ref))):
            cols = slice(i * d, (i + 1) * d)
            gate = jax.nn.sigmoid(_dot(h, wg_ref[0, :, cols]) + bg_ref[:, cols])
            part = gate * _dot(act, w_ref[...])
            merged = part if merged is None else merged + part
        ys.append(_dot(merged.astype(BF16), wo_ref[...]))
    for rows, x, y in zip(groups, xs, ys):
        o_ref[rows, :] = x + _rms(y, gpost_ref[...])


def _merge(x2, attn, u, seq_len, gpre, w_in, bg, wa, wp, wc, wo, gpost, wpg, ps, wdw, bdw, lng, lnb, l):
    m, d = x2.shape
    gate_w = bg.shape[2]
    pool_w, conv_w = wp.shape[1], wc.shape[1]
    tile = lambda w: pl.BlockSpec((MERGE_TILE, w), lambda i: (i, 0))
    layer = lambda a: _const_spec((None,) + a.shape[1:], (l, 0, 0))
    gate_cols = pl.BlockSpec((pl.Element(1), pl.Element(d), pl.Element(gate_w)),
                             lambda i: (l, 0, w_in.shape[2] - gate_w), pipeline_mode=pl.Buffered(1))
    kern = functools.partial(_merge_kernel, tiles_per_seq=seq_len // MERGE_TILE,
                             pool_w=pool_w, conv_w=conv_w)
    return pl.pallas_call(
        kern,
        grid=(m // MERGE_TILE,),
        in_specs=[
            tile(d), tile(attn.shape[1]), tile(u.shape[1]),
            layer(gpre), gate_cols, layer(bg), layer(wa), layer(wp), layer(wc), layer(wo),
            layer(gpost), layer(wpg), layer(ps), layer(wdw), layer(bdw), layer(lng), layer(lnb),
        ],
        out_specs=tile(d),
        out_shape=jax.ShapeDtypeStruct((m, d), F32),
        scratch_shapes=[
            pltpu.VMEM((SUBLANES, SEQ_PAD + MERGE_TILE + SEQ_PAD, pool_w), F32),
            pltpu.VMEM((SUBLANES, SEQ_PAD + MERGE_TILE + SEQ_PAD, conv_w), F32),
        ],
        compiler_params=_params("arbitrary"),
        name="merge",
    )(x2, attn, u, gpre, w_in, bg, wa, wp, wc, wo, gpost, wpg, ps, wdw, bdw, lng, lnb)


def _ffn_kernel(x_ref, gpre_ref, win_ref, wout_ref, gpost_ref, o_ref, *, d_ff):
    groups = _row_groups(x_ref.shape[0], FFN_SUBTILES)
    xs = [x_ref[rows, :] for rows in groups]
    hs = [_rms(x, gpre_ref[...]).astype(BF16) for x in xs]
    fs = []
    for h in hs:
        f = None
        for c in range(0, d_ff, FFN_CHUNK):
            w = min(FFN_CHUNK, d_ff - c)
            gt = _dot(h, win_ref[:, c:c + w])
            up = _dot(h, win_ref[:, d_ff + c:d_ff + c + w])
            act = (gt * jax.nn.sigmoid(gt) * up).astype(BF16)
            part = _dot(act, wout_ref[c:c + w, :])
            f = part if f is None else f + part
        fs.append(f)
    for rows, x, f in zip(groups, xs, fs):
        o_ref[rows, :] = x + _rms(f, gpost_ref[...])


def _ffn(x2, gpre, win, wout, gpost, l):
    m, d = x2.shape
    d_ff = wout.shape[1]
    kern = functools.partial(_ffn_kernel, d_ff=d_ff)
    return pl.pallas_call(
        kern,
        grid=(m // TOKEN_TILE,),
        in_specs=[
            pl.BlockSpec((TOKEN_TILE, d), lambda i: (i, 0)),
            _const_spec((None, 1, d), (l, 0, 0)),
            _const_spec((None, d, 2 * d_ff), (l, 0, 0)),
            _const_spec((None, d_ff, d), (l, 0, 0)),
            _const_spec((None, 1, d), (l, 0, 0)),
        ],
        out_specs=pl.BlockSpec((TOKEN_TILE, d), lambda i: (i, 0)),
        out_shape=jax.ShapeDtypeStruct((m, d), F32),
        compiler_params=_params("arbitrary"),
        name="ffn",
    )(x2, gpre, win, wout, gpost)


def _block_diag(w):
    depth, groups, gd, _ = w.shape
    eye = jnp.eye(groups, dtype=w.dtype)
    return jnp.einsum('lgcd,gh->lgchd', w, eye).reshape(depth, groups * gd, groups * gd)


def kernel(x, pre_mix_g, w_in, b_gate, w_attn_o, rel_bias, w_pool_g, pool_scale, w_pool_o, w_dw,
           b_dw, conv_ln_g, conv_ln_b, w_conv_o, w_out, post_mix_g, pre_ffn_g, w_ffn_in,
           w_ffn_out, post_ffn_g):
    b, s, d = x.shape
    depth = w_in.shape[0]
    attn_w = w_attn_o.shape[1]
    pool_w = w_pool_o.shape[1]
    conv_w = w_conv_o.shape[1]
    u_w = pool_w + 2 * conv_w

    row = lambda a: a.reshape(depth, 1, a.shape[-1])
    w_in_b = w_in.astype(BF16)
    wa_b, wp_b, wc_b, wo_b = (w.astype(BF16) for w in (w_attn_o, w_pool_o, w_conv_o, w_out))
    wfi_b, wfo_b = w_ffn_in.astype(BF16), w_ffn_out.astype(BF16)
    wg_bd = _block_diag(w_pool_g).astype(BF16)
    bias0_t, bias1_t = _bias_tables(rel_bias)

    x2 = x.reshape(b * s, d)
    for l in range(depth):
        qkv, u = _inproj(x2, row(pre_mix_g), w_in_b, l, attn_w, u_w)
        attn = _moba(qkv.reshape(b, s, 3 * attn_w), bias0_t, bias1_t, attn_w)
        x2 = _merge(x2, attn.reshape(b * s, attn_w), u, s, row(pre_mix_g), w_in_b, row(b_gate),
                    wa_b, wp_b, wc_b, wo_b, row(post_mix_g), wg_bd, row(pool_scale), w_dw,
                    row(b_dw), row(conv_ln_g), row(conv_ln_b), l)
        x2 = _ffn(x2, row(pre_ffn_g), wfi_b, wfo_b, row(post_ffn_g), l)
    return x2.reshape(b, s, d)
```

```python
import functools
import math

import jax
import jax.numpy as jnp
from jax import lax
from jax.experimental import pallas as pl
from jax.experimental.pallas import tpu as pltpu

HEAD_DIM = 64
MOBA_BLOCK = 256
MOBA_TOPK = 3
POOL_WINDOWS = (2, 4, 8, 16)
CONV_KERNEL = 31
REL_BUCKETS = 32
REL_MAX_DIST = 128
EPS = 1e-6
NEG = -1e30
LOG2E = math.log2(math.e)

LANES = 128
HEADS_PER_STEP = LANES // HEAD_DIM
TOKEN_TILE = 1024
INPROJ_SUBTILES = 2
MERGE_TILE = 512
FFN_SUBTILES = 4
FFN_CHUNK = 1024
VMEM_LIMIT = 56 * 1024 * 1024
SUBLANES = 8
SEQ_PAD = 32
SEQ_TILE = 256
ONES_ROWS = 16
MOBA_AHEAD = 3
MOBA_BUFFERS = 8

F32 = jnp.float32
BF16 = jnp.bfloat16


def _params(*sem):
    return pltpu.CompilerParams(dimension_semantics=sem, vmem_limit_bytes=VMEM_LIMIT)


def _const_spec(shape, index):
    return pl.BlockSpec(shape, lambda *_: index, pipeline_mode=pl.Buffered(1))


def _rms(x, g):
    return x * lax.rsqrt(jnp.mean(x * x, axis=-1, keepdims=True) + EPS) * g


def _dot(a, b):
    return jnp.dot(a, b, preferred_element_type=F32)


def _row_groups(n_rows, n_groups):
    size = n_rows // n_groups
    return [slice(i * size, (i + 1) * size) for i in range(n_groups)]


def _inproj_kernel(x_ref, g_ref, w_ref, qkv_ref, u_ref, *, attn_w, chunk):
    qkv_w = 3 * attn_w
    groups = _row_groups(x_ref.shape[0], INPROJ_SUBTILES)
    hs = [_rms(x_ref[rows, :], g_ref[...]).astype(BF16) for rows in groups]
    for rows, h in zip(groups, hs):
        for c in range(0, qkv_w, chunk):
            r = _dot(h, w_ref[:, c:c + chunk])
            if c < attn_w:
                r = r * (HEAD_DIM ** -0.5)
            qkv_ref[rows, c:c + chunk] = r.astype(BF16)
        u_ref[rows, :] = _dot(h, w_ref[:, qkv_w:])


def _inproj(x2, g, w, l, attn_w, u_w):
    m, d = x2.shape
    kern = functools.partial(_inproj_kernel, attn_w=attn_w, chunk=512)
    return pl.pallas_call(
        kern,
        grid=(m // TOKEN_TILE,),
        in_specs=[
            pl.BlockSpec((TOKEN_TILE, d), lambda i: (i, 0)),
            _const_spec((None, 1, d), (l, 0, 0)),
            _const_spec((None, d, 3 * attn_w + u_w), (l, 0, 0)),
        ],
        out_specs=[
            pl.BlockSpec((TOKEN_TILE, 3 * attn_w), lambda i: (i, 0)),
            pl.BlockSpec((TOKEN_TILE, u_w), lambda i: (i, 0)),
        ],
        out_shape=[
            jax.ShapeDtypeStruct((m, 3 * attn_w), BF16),
            jax.ShapeDtypeStruct((m, u_w), F32),
        ],
        compiler_params=_params("arbitrary"),
        name="inproj",
    )(x2, g, w)


def _moba_kernel(q_ref, k_ref, v_ref, b0_ref, b1_ref, o_ref,
                 kmean_ref, vt_ref, *bufs, n_blocks):
    blk = MOBA_BLOCK
    for j in range(n_blocks):
        kb = k_ref[0, j * blk:(j + 1) * blk, :].astype(F32)
        kmean_ref[j:j + 1, :] = jnp.mean(kb, axis=0, keepdims=True)
        vt = v_ref[0, j * blk:(j + 1) * blk, :].astype(F32).T
        for h in range(HEADS_PER_STEP):
            vt_ref[h, 0:HEAD_DIM, j * blk:(j + 1) * blk] = (
                vt[h * HEAD_DIM:(h + 1) * HEAD_DIM].astype(BF16))
    for h in range(HEADS_PER_STEP):
        vt_ref[h, HEAD_DIM:HEAD_DIM + ONES_ROWS, :] = jnp.ones((ONES_ROWS, vt_ref.shape[2]), BF16)

    dim_row = lax.broadcasted_iota(jnp.int32, (LANES, blk), 0)
    blk_row = lax.broadcasted_iota(jnp.int32, (n_blocks, blk), 0)
    kmean = kmean_ref[...]

    n_buf = len(bufs) // 2
    s_refs, p_refs = bufs[:n_buf], bufs[n_buf:]
    qts = {}

    def pass1(qi, h, s_ref):
        if qi not in qts:
            qts[qi] = q_ref[0, qi * blk:(qi + 1) * blk, :].astype(F32).T
        in_head = (dim_row >= h * HEAD_DIM) & (dim_row < (h + 1) * HEAD_DIM)
        qt_h = jnp.where(in_head, qts[qi], 0.0)
        qt_hb = (qt_h * LOG2E).astype(BF16)

        gate = jnp.dot(kmean, qt_h, preferred_element_type=F32,
                       precision=lax.Precision.HIGHEST)
        rank = jnp.zeros((n_blocks, blk), jnp.int32)
        for j in range(qi):
            other = gate[j:j + 1, :]
            beats = (other > gate) | ((other == gate) & (j < blk_row))
            rank = rank + jnp.where(beats, 1, 0)
        sel = (blk_row < qi) & (rank < MOBA_TOPK)

        col_max = []
        n_far = max(qi - 1, 0)
        if n_far:
            r = _dot(k_ref[0, 0:n_far * blk, :], qt_hb)
            s_ref[0:n_far * blk, :] = r
            for j in range(n_far):
                col_max.append(jnp.max(r[j * blk:(j + 1) * blk], axis=0, keepdims=True))
        if qi >= 1:
            lo = (qi - 1) * blk
            r = _dot(k_ref[0, lo:lo + blk, :], qt_hb) + b1_ref[h]
            s_ref[lo:lo + blk, :] = r
            col_max.append(jnp.max(r, axis=0, keepdims=True))
        lo = qi * blk
        r = _dot(k_ref[0, lo:lo + blk, :], qt_hb) + b0_ref[h]
        s_ref[lo:lo + blk, :] = r
        m = jnp.max(r, axis=0, keepdims=True)
        chosen = [sel[j:j + 1, :] for j in range(qi)]
        for j in range(qi):
            m = jnp.maximum(m, jnp.where(chosen[j], col_max[j], NEG))
        return [jnp.where(chosen[j], m, -NEG) for j in range(qi)] + [m]

    def pass2(qi, h, offs, s_ref, p_ref):
        for j in range(qi + 1):
            x = s_ref[j * blk:(j + 1) * blk, :] - offs[j]
            p_ref[j * blk:(j + 1) * blk, :] = jnp.exp2(x.astype(BF16))
        n_keys = (qi + 1) * blk
        acc = _dot(vt_ref[h, :, 0:n_keys], p_ref[0:n_keys, :])
        return acc[0:HEAD_DIM] / acc[HEAD_DIM:HEAD_DIM + 1]

    units = [(qi, h) for qi in reversed(range(n_blocks)) for h in range(HEADS_PER_STEP)]
    offs, outs = {}, {}
    for n in range(len(units) + MOBA_AHEAD):
        if n < len(units):
            offs[n] = pass1(*units[n], s_refs[n % n_buf])
        if n >= MOBA_AHEAD:
            d = n - MOBA_AHEAD
            qi, h = units[d]
            outs[h] = pass2(qi, h, offs.pop(d), s_refs[d % n_buf], p_refs[d % n_buf])
            if h == HEADS_PER_STEP - 1:
                both = jnp.concatenate([outs[i] for i in range(HEADS_PER_STEP)], axis=0)
                o_ref[0, qi * blk:(qi + 1) * blk, :] = both.T.astype(o_ref.dtype)


def _moba(qkv3, bias0_t, bias1_t, attn_w):
    b, s, _ = qkv3.shape
    n_heads = attn_w // HEAD_DIM
    n_blocks = s // MOBA_BLOCK
    hp = n_heads // HEADS_PER_STEP
    cols = attn_w // LANES
    n_buf = MOBA_BUFFERS
    kern = functools.partial(_moba_kernel, n_blocks=n_blocks)
    bias_spec = pl.BlockSpec((HEADS_PER_STEP, MOBA_BLOCK, MOBA_BLOCK), lambda p, i: (p, 0, 0))
    return pl.pallas_call(
        kern,
        grid=(hp, b),
        in_specs=[
            pl.BlockSpec((1, s, LANES), lambda p, i: (i, 0, p)),
            pl.BlockSpec((1, s, LANES), lambda p, i: (i, 0, cols + p)),
            pl.BlockSpec((1, s, LANES), lambda p, i: (i, 0, 2 * cols + p)),
            bias_spec,
            bias_spec,
        ],
        out_specs=pl.BlockSpec((1, s, LANES), lambda p, i: (i, 0, p)),
        out_shape=jax.ShapeDtypeStruct((b, s, attn_w), BF16),
        scratch_shapes=[
            pltpu.VMEM((n_blocks, LANES), F32),
            pltpu.VMEM((HEADS_PER_STEP, HEAD_DIM + ONES_ROWS, s), BF16),
        ] + [pltpu.VMEM((s, MOBA_BLOCK), F32)] * n_buf + [pltpu.VMEM((s, MOBA_BLOCK), BF16)] * n_buf,
        compiler_params=_params("arbitrary", "arbitrary"),
        name="moba",
    )(qkv3, qkv3, qkv3, bias0_t, bias1_t)


def _t5_bucket(rel):
    n = jnp.maximum(rel, 0)
    max_exact = REL_BUCKETS // 2
    nf = jnp.maximum(n, 1).astype(F32)
    large = max_exact + (jnp.log(nf / max_exact) / math.log(REL_MAX_DIST / max_exact)
                         * (REL_BUCKETS - max_exact)).astype(jnp.int32)
    large = jnp.minimum(large, REL_BUCKETS - 1)
    return jnp.where(n < max_exact, n, large)


def _bias_tables(rel_bias):
    assert MOBA_BLOCK + 1 >= REL_MAX_DIST
    table_t = rel_bias.T.astype(F32)
    offs = jnp.arange(MOBA_BLOCK)
    rel_own = offs[None, :] - offs[:, None]
    buckets = jnp.arange(REL_BUCKETS)[:, None, None]

    def lookup(rel):
        onehot = (_t5_bucket(rel)[None] == buckets).astype(F32)
        return jnp.einsum('hb,bkq->hkq', table_t, onehot, precision=lax.Precision.HIGHEST)

    bias0 = jnp.where(rel_own >= 0, lookup(rel_own), NEG)
    bias1 = lookup(rel_own + MOBA_BLOCK)
    far = _t5_bucket(jnp.int32(2 * MOBA_BLOCK - (MOBA_BLOCK - 1)))
    bias_far = table_t[:, far][:, None, None]
    return (bias0 - bias_far) * LOG2E, (bias1 - bias_far) * LOG2E


def _pool_lane_tiles(pool_w):
    group_w = pool_w // len(POOL_WINDOWS)
    tiles = []
    for lo in range(0, pool_w, LANES):
        windows = [w for gi, w in enumerate(POOL_WINDOWS) if lo <= gi * group_w < lo + LANES]
        lane_group = (lax.broadcasted_iota(jnp.int32, (SEQ_TILE, LANES), 1) + lo) // group_w
        window = jnp.zeros((SEQ_TILE, LANES), F32)
        for gi, w in enumerate(POOL_WINDOWS):
            window = jnp.where(lane_group == gi, float(w), window)
        tiles.append((slice(lo, lo + LANES), windows, window))
    return tiles


def _merge_kernel(x_ref, attn_ref, u_ref, gpre_ref, wg_ref, bg_ref, wa_ref, wp_ref, wc_ref, wo_ref,
                  gpost_ref, wpg_ref, ps_ref, wdw_ref, bdw_ref, lng_ref, lnb_ref, o_ref,
                  prot_ref, crot_ref, *, tiles_per_seq, pool_w, conv_w):
    d = x_ref.shape[1]
    n_rows = x_ref.shape[0]
    step = pl.program_id(0)
    tile_in_seq = step % tiles_per_seq
    seq_start = tile_in_seq == 0
    lane_tiles = _pool_lane_tiles(pool_w)
    conv_lanes = slice(0, conv_w)
    copies = [(prot_ref, lanes, min(SUBLANES, max(windows))) for lanes, windows, _ in lane_tiles]
    copies.append((crot_ref, conv_lanes, SUBLANES))

    @pl.when(step == 0)
    def _first_halo_source():
        for ref, lanes, n_copies in copies:
            for r in range(n_copies):
                ref[r, n_rows:n_rows + SEQ_PAD, lanes] = jnp.zeros((SEQ_PAD, lanes.stop - lanes.start), F32)

    for ref, lanes, n_copies in copies:
        for r in range(n_copies):
            tail = ref[r, n_rows:n_rows + SEQ_PAD, lanes]
            ref[r, 0:SEQ_PAD, lanes] = jnp.where(seq_start, 0.0, tail)

    def delayed(ref, delay, t0, lanes):
        a8, r = delay // SUBLANES * SUBLANES, delay % SUBLANES
        return ref[r, pl.ds(SEQ_PAD + t0 - a8, SEQ_TILE), lanes]

    for t0 in range(0, n_rows, SEQ_TILE):
        rows = pl.ds(t0, SEQ_TILE)
        here = pl.ds(SEQ_PAD + t0, SEQ_TILE)
        prot_ref[0, here, 0:pool_w] = u_ref[rows, 0:pool_w]
        a = u_ref[rows, pool_w:pool_w + conv_w]
        gt = u_ref[rows, pool_w + conv_w:pool_w + 2 * conv_w]
        crot_ref[0, here, conv_lanes] = a * jax.nn.sigmoid(gt)
        for ref, lanes, n_copies in copies:
            for r in range(1, n_copies):
                ref[r, here, lanes] = ref[0, pl.ds(SEQ_PAD + t0 - r, SEQ_TILE), lanes]

    row = lax.broadcasted_iota(jnp.int32, (SEQ_TILE, LANES), 0)

    def pool_branch(t0):
        parts = []
        for lanes, windows, window in lane_tiles:
            tok = delayed(prot_ref, 0, t0, lanes)
            run = tok
            wsum = jnp.zeros_like(tok)
            for k in range(1, max(windows)):
                run = run + delayed(prot_ref, k, t0, lanes)
                if k + 1 in windows:
                    wsum = jnp.where(window == float(k + 1), run, wsum)
            seen = (row + (tile_in_seq * n_rows + t0 + 1)).astype(F32)
            parts.append(wsum / jnp.minimum(seen, window) - tok)
        y = _dot(jnp.concatenate(parts, axis=1).astype(BF16), wpg_ref[...]) * ps_ref[...]
        return y.astype(BF16)

    def conv_branch(t0):
        acc = jnp.zeros((SEQ_TILE, conv_w), F32) + bdw_ref[...]
        for k in range(CONV_KERNEL):
            acc = acc + delayed(crot_ref, CONV_KERNEL - 1 - k, t0, conv_lanes) * wdw_ref[k:k + 1, :]
        mu = jnp.mean(acc, axis=-1, keepdims=True)
        cen = acc - mu
        var = jnp.mean(cen * cen, axis=-1, keepdims=True)
        yn = cen * lax.rsqrt(var + EPS) * lng_ref[...] + lnb_ref[...]
        return (yn * jax.nn.sigmoid(yn)).astype(BF16)

    groups = [slice(t0, t0 + SEQ_TILE) for t0 in range(0, n_rows, SEQ_TILE)]
    xs = [x_ref[rows, :] for rows in groups]
    hs = [_rms(x, gpre_ref[...]).astype(BF16) for x in xs]
    ys = []
    for rows, h in zip(groups, hs):
        acts = (attn_ref[rows, :], pool_branch(rows.start), conv_branch(rows.start))
        merged = None
        for i, (act, w_ref) in enumerate(zip(acts, (wa_ref, wp_ref, wc_ref))):
            cols = slice(i * d, (i + 1) * d)
            gate = jax.nn.sigmoid(_dot(h, wg_ref[0, :, cols]) + bg_ref[:, cols])
            part = gate * _dot(act, w_ref[...])
            merged = part if merged is None else merged + part
        ys.append(_dot(merged.astype(BF16), wo_ref[...]))
    for rows, x, y in zip(groups, xs, ys):
        o_ref[rows, :] = x + _rms(y, gpost_ref[...])


def _merge(x2, attn, u, seq_len, gpre, w_in, bg, wa, wp, wc, wo, gpost, wpg, ps, wdw, bdw, lng, lnb, l):
    m, d = x2.shape
    gate_w = bg.shape[2]
    pool_w, conv_w = wp.shape[1], wc.shape[1]
    tile = lambda w: pl.BlockSpec((MERGE_TILE, w), lambda i: (i, 0))
    layer = lambda a: _const_spec((None,) + a.shape[1:], (l, 0, 0))
    gate_cols = pl.BlockSpec((pl.Element(1), pl.Element(d), pl.Element(gate_w)),
                             lambda i: (l, 0, w_in.shape[2] - gate_w), pipeline_mode=pl.Buffered(1))
    kern = functools.partial(_merge_kernel, tiles_per_seq=seq_len // MERGE_TILE,
                             pool_w=pool_w, conv_w=conv_w)
    return pl.pallas_call(
        kern,
        grid=(m // MERGE_TILE,),
        in_specs=[
            tile(d), tile(attn.shape[1]), tile(u.shape[1]),
            layer(gpre), gate_cols, layer(bg), layer(wa), layer(wp), layer(wc), layer(wo),
            layer(gpost), layer(wpg), layer(ps), layer(wdw), layer(bdw), layer(lng), layer(lnb),
        ],
        out_specs=tile(d),
        out_shape=jax.ShapeDtypeStruct((m, d), F32),
        scratch_shapes=[
            pltpu.VMEM((SUBLANES, SEQ_PAD + MERGE_TILE + SEQ_PAD, pool_w), F32),
            pltpu.VMEM((SUBLANES, SEQ_PAD + MERGE_TILE + SEQ_PAD, conv_w), F32),
        ],
        compiler_params=_params("arbitrary"),
        name="merge",
    )(x2, attn, u, gpre, w_in, bg, wa, wp, wc, wo, gpost, wpg, ps, wdw, bdw, lng, lnb)


def _ffn_kernel(x_ref, gpre_ref, win_ref, wout_ref, gpost_ref, o_ref, *, d_ff):
    groups = _row_groups(x_ref.shape[0], FFN_SUBTILES)
    xs = [x_ref[rows, :] for rows in groups]
    hs = [_rms(x, gpre_ref[...]).astype(BF16) for x in xs]
    fs = []
    for h in hs:
        f = None
        for c in range(0, d_ff, FFN_CHUNK):
            w = min(FFN_CHUNK, d_ff - c)
            gt = _dot(h, win_ref[:, c:c + w])
            up = _dot(h, win_ref[:, d_ff + c:d_ff + c + w])
            act = (gt * jax.nn.sigmoid(gt) * up).astype(BF16)
            part = _dot(act, wout_ref[c:c + w, :])
            f = part if f is None else f + part
        fs.append(f)
    for rows, x, f in zip(groups, xs, fs):
        o_ref[rows, :] = x + _rms(f, gpost_ref[...])


def _ffn(x2, gpre, win, wout, gpost, l):
    m, d = x2.shape
    d_ff = wout.shape[1]
    kern = functools.partial(_ffn_kernel, d_ff=d_ff)
    return pl.pallas_call(
        kern,
        grid=(m // TOKEN_TILE,),
        in_specs=[
            pl.BlockSpec((TOKEN_TILE, d), lambda i: (i, 0)),
            _const_spec((None, 1, d), (l, 0, 0)),
            _const_spec((None, d, 2 * d_ff), (l, 0, 0)),
            _const_spec((None, d_ff, d), (l, 0, 0)),
            _const_spec((None, 1, d), (l, 0, 0)),
        ],
        out_specs=pl.BlockSpec((TOKEN_TILE, d), lambda i: (i, 0)),
        out_shape=jax.ShapeDtypeStruct((m, d), F32),
        compiler_params=_params("arbitrary"),
        name="ffn",
    )(x2, gpre, win, wout, gpost)


def _block_diag(w):
    depth, groups, gd, _ = w.shape
    eye = jnp.eye(groups, dtype=w.dtype)
    return jnp.einsum('lgcd,gh->lgchd', w, eye).reshape(depth, groups * gd, groups * gd)


def kernel(x, pre_mix_g, w_in, b_gate, w_attn_o, rel_bias, w_pool_g, pool_scale, w_pool_o, w_dw,
           b_dw, conv_ln_g, conv_ln_b, w_conv_o, w_out, post_mix_g, pre_ffn_g, w_ffn_in,
           w_ffn_out, post_ffn_g):
    b, s, d = x.shape
    depth = w_in.shape[0]
    attn_w = w_attn_o.shape[1]
    pool_w = w_pool_o.shape[1]
    conv_w = w_conv_o.shape[1]
    u_w = pool_w + 2 * conv_w

    row = lambda a: a.reshape(depth, 1, a.shape[-1])
    w_in_b = w_in.astype(BF16)
    wa_b, wp_b, wc_b, wo_b = (w.astype(BF16) for w in (w_attn_o, w_pool_o, w_conv_o, w_out))
    wfi_b, wfo_b = w_ffn_in.astype(BF16), w_ffn_out.astype(BF16)
    wg_bd = _block_diag(w_pool_g).astype(BF16)
    bias0_t, bias1_t = _bias_tables(rel_bias)

    x2 = x.reshape(b * s, d)
    for l in range(depth):
        qkv, u = _inproj(x2, row(pre_mix_g), w_in_b, l, attn_w, u_w)
        attn = _moba(qkv.reshape(b, s, 3 * attn_w), bias0_t, bias1_t, attn_w)
        x2 = _merge(x2, attn.reshape(b * s, attn_w), u, s, row(pre_mix_g), w_in_b, row(b_gate),
                    wa_b, wp_b, wc_b, wo_b, row(post_mix_g), wg_bd, row(pool_scale), w_dw,
                    row(b_dw), row(conv_ln_g), row(conv_ln_b), l)
        x2 = _ffn(x2, row(pre_ffn_g), wfi_b, wfo_b, row(post_ffn_g), l)
    return x2.reshape(b, s, d)
```

```python
import functools
import math

import jax
import jax.numpy as jnp
from jax import lax
from jax.experimental import pallas as pl
from jax.experimental.pallas import tpu as pltpu

HEAD_DIM = 64
MOBA_BLOCK = 256
MOBA_TOPK = 3
POOL_WINDOWS = (2, 4, 8, 16)
CONV_KERNEL = 31
REL_BUCKETS = 32
REL_MAX_DIST = 128
EPS = 1e-6
NEG = -1e30
LOG2E = math.log2(math.e)

LANES = 128
HEADS_PER_STEP = LANES // HEAD_DIM
TOKEN_TILE = 1024
INPROJ_SUBTILES = 2
MERGE_TILE = 512
FFN_SUBTILES = 4
FFN_CHUNK = 1024
VMEM_LIMIT = 56 * 1024 * 1024
SUBLANES = 8
SEQ_PAD = 32
SEQ_TILE = 256
ONES_ROWS = 16
MOBA_AHEAD = 3
MOBA_BUFFERS = 8

F32 = jnp.float32
BF16 = jnp.bfloat16


def _params(*sem):
    return pltpu.CompilerParams(dimension_semantics=sem, vmem_limit_bytes=VMEM_LIMIT)


def _const_spec(shape, index):
    return pl.BlockSpec(shape, lambda *_: index, pipeline_mode=pl.Buffered(1))


def _rms(x, g):
    return x * lax.rsqrt(jnp.mean(x * x, axis=-1, keepdims=True) + EPS) * g


def _dot(a, b):
    return jnp.dot(a, b, preferred_element_type=F32)


def _row_groups(n_rows, n_groups):
    size = n_rows // n_groups
    return [slice(i * size, (i + 1) * size) for i in range(n_groups)]


def _inproj_kernel(x_ref, g_ref, w_ref, qkv_ref, u_ref, *, attn_w, chunk):
    qkv_w = 3 * attn_w
    groups = _row_groups(x_ref.shape[0], INPROJ_SUBTILES)
    hs = [_rms(x_ref[rows, :], g_ref[...]).astype(BF16) for rows in groups]
    for rows, h in zip(groups, hs):
        for c in range(0, qkv_w, chunk):
            r = _dot(h, w_ref[:, c:c + chunk])
            if c < attn_w:
                r = r * (HEAD_DIM ** -0.5)
            qkv_ref[rows, c:c + chunk] = r.astype(BF16)
        u_ref[rows, :] = _dot(h, w_ref[:, qkv_w:])


def _inproj(x2, g, w, l, attn_w, u_w):
    m, d = x2.shape
    kern = functools.partial(_inproj_kernel, attn_w=attn_w, chunk=512)
    return pl.pallas_call(
        kern,
        grid=(m // TOKEN_TILE,),
        in_specs=[
            pl.BlockSpec((TOKEN_TILE, d), lambda i: (i, 0)),
            _const_spec((None, 1, d), (l, 0, 0)),
            _const_spec((None, d, 3 * attn_w + u_w), (l, 0, 0)),
        ],
        out_specs=[
            pl.BlockSpec((TOKEN_TILE, 3 * attn_w), lambda i: (i, 0)),
            pl.BlockSpec((TOKEN_TILE, u_w), lambda i: (i, 0)),
        ],
        out_shape=[
            jax.ShapeDtypeStruct((m, 3 * attn_w), BF16),
            jax.ShapeDtypeStruct((m, u_w), F32),
        ],
        compiler_params=_params("arbitrary"),
        name="inproj",
    )(x2, g, w)


def _moba_kernel(q_ref, k_ref, v_ref, b0_ref, b1_ref, o_ref,
                 kmean_ref, vt_ref, *bufs, n_blocks):
    blk = MOBA_BLOCK
    for j in range(n_blocks):
        kb = k_ref[0, j * blk:(j + 1) * blk, :].astype(F32)
        kmean_ref[j:j + 1, :] = jnp.mean(kb, axis=0, keepdims=True)
        vt = v_ref[0, j * blk:(j + 1) * blk, :].astype(F32).T
        for h in range(HEADS_PER_STEP):
            vt_ref[h, 0:HEAD_DIM, j * blk:(j + 1) * blk] = (
                vt[h * HEAD_DIM:(h + 1) * HEAD_DIM].astype(BF16))
    for h in range(HEADS_PER_STEP):
        vt_ref[h, HEAD_DIM:HEAD_DIM + ONES_ROWS, :] = jnp.ones((ONES_ROWS, vt_ref.shape[2]), BF16)

    dim_row = lax.broadcasted_iota(jnp.int32, (LANES, blk), 0)
    blk_row = lax.broadcasted_iota(jnp.int32, (n_blocks, blk), 0)
    dim_lane = lax.broadcasted_iota(jnp.int32, (n_blocks, LANES), 1)
    kmean_heads = jnp.concatenate(
        [jnp.where((dim_lane >= h * HEAD_DIM) & (dim_lane < (h + 1) * HEAD_DIM), kmean_ref[...], 0.0)
         for h in range(HEADS_PER_STEP)], axis=0)

    n_buf = len(bufs) // 2
    s_refs, p_refs = bufs[:n_buf], bufs[n_buf:]
    qts, gates = {}, {}

    def pass1(qi, h, s_ref):
        if qi not in qts:
            qts[qi] = q_ref[0, qi * blk:(qi + 1) * blk, :].astype(F32).T
            if qi > MOBA_TOPK:
                gates[qi] = jnp.dot(kmean_heads, qts[qi], preferred_element_type=F32,
                                    precision=lax.Precision.HIGHEST)
        in_head = (dim_row >= h * HEAD_DIM) & (dim_row < (h + 1) * HEAD_DIM)
        qt_hb = (jnp.where(in_head, qts[qi], 0.0) * LOG2E).astype(BF16)

        chosen = None
        if qi > MOBA_TOPK:
            gate = gates[qi][h * n_blocks:(h + 1) * n_blocks]
            rank = jnp.zeros((n_blocks, blk), jnp.int32)
            for j in range(qi):
                other = gate[j:j + 1, :]
                beats = (other > gate) | ((other == gate) & (j < blk_row))
                rank = rank + jnp.where(beats, 1, 0)
            sel = (blk_row < qi) & (rank < MOBA_TOPK)
            chosen = [sel[j:j + 1, :] for j in range(qi)]

        col_max = []
        n_far = max(qi - 1, 0)
        if n_far:
            r = _dot(k_ref[0, 0:n_far * blk, :], qt_hb)
            s_ref[0:n_far * blk, :] = r
            for j in range(n_far):
                col_max.append(jnp.max(r[j * blk:(j + 1) * blk], axis=0, keepdims=True))
        if qi >= 1:
            lo = (qi - 1) * blk
            r = _dot(k_ref[0, lo:lo + blk, :], qt_hb) + b1_ref[h]
            s_ref[lo:lo + blk, :] = r
            col_max.append(jnp.max(r, axis=0, keepdims=True))
        lo = qi * blk
        r = _dot(k_ref[0, lo:lo + blk, :], qt_hb) + b0_ref[h]
        s_ref[lo:lo + blk, :] = r
        m = jnp.max(r, axis=0, keepdims=True)
        if chosen is None:
            for j in range(qi):
                m = jnp.maximum(m, col_max[j])
            return [m] * (qi + 1)
        for j in range(qi):
            m = jnp.maximum(m, jnp.where(chosen[j], col_max[j], NEG))
        return [jnp.where(chosen[j], m, -NEG) for j in range(qi)] + [m]

    def pass2(qi, h, offs, s_ref, p_ref):
        for j in range(qi + 1):
            x = s_ref[j * blk:(j + 1) * blk, :] - offs[j]
            p_ref[j * blk:(j + 1) * blk, :] = jnp.exp2(x.astype(BF16))
        n_keys = (qi + 1) * blk
        acc = _dot(vt_ref[h, :, 0:n_keys], p_ref[0:n_keys, :])
        return acc[0:HEAD_DIM] / acc[HEAD_DIM:HEAD_DIM + 1]

    units = [(qi, h) for qi in reversed(range(n_blocks)) for h in range(HEADS_PER_STEP)]
    offs, outs = {}, {}
    for n in range(len(units) + MOBA_AHEAD):
        if n < len(units):
            offs[n] = pass1(*units[n], s_refs[n % n_buf])
        if n >= MOBA_AHEAD:
            d = n - MOBA_AHEAD
            qi, h = units[d]
            outs[h] = pass2(qi, h, offs.pop(d), s_refs[d % n_buf], p_refs[d % n_buf])
            if h == HEADS_PER_STEP - 1:
                both = jnp.concatenate([outs[i] for i in range(HEADS_PER_STEP)], axis=0)
                o_ref[0, qi * blk:(qi + 1) * blk, :] = both.T.astype(o_ref.dtype)


def _moba(qkv3, bias0_t, bias1_t, attn_w):
    b, s, _ = qkv3.shape
    n_heads = attn_w // HEAD_DIM
    n_blocks = s // MOBA_BLOCK
    hp = n_heads // HEADS_PER_STEP
    cols = attn_w // LANES
    n_buf = MOBA_BUFFERS
    kern = functools.partial(_moba_kernel, n_blocks=n_blocks)
    bias_spec = pl.BlockSpec((HEADS_PER_STEP, MOBA_BLOCK, MOBA_BLOCK), lambda p, i: (p, 0, 0))
    return pl.pallas_call(
        kern,
        grid=(hp, b),
        in_specs=[
            pl.BlockSpec((1, s, LANES), lambda p, i: (i, 0, p)),
            pl.BlockSpec((1, s, LANES), lambda p, i: (i, 0, cols + p)),
            pl.BlockSpec((1, s, LANES), lambda p, i: (i, 0, 2 * cols + p)),
            bias_spec,
            bias_spec,
        ],
        out_specs=pl.BlockSpec((1, s, LANES), lambda p, i: (i, 0, p)),
        out_shape=jax.ShapeDtypeStruct((b, s, attn_w), BF16),
        scratch_shapes=[
            pltpu.VMEM((n_blocks, LANES), F32),
            pltpu.VMEM((HEADS_PER_STEP, HEAD_DIM + ONES_ROWS, s), BF16),
        ] + [pltpu.VMEM((s, MOBA_BLOCK), F32)] * n_buf + [pltpu.VMEM((s, MOBA_BLOCK), BF16)] * n_buf,
        compiler_params=_params("arbitrary", "arbitrary"),
        name="moba",
    )(qkv3, qkv3, qkv3, bias0_t, bias1_t)


def _t5_bucket(rel):
    n = jnp.maximum(rel, 0)
    max_exact = REL_BUCKETS // 2
    nf = jnp.maximum(n, 1).astype(F32)
    large = max_exact + (jnp.log(nf / max_exact) / math.log(REL_MAX_DIST / max_exact)
                         * (REL_BUCKETS - max_exact)).astype(jnp.int32)
    large = jnp.minimum(large, REL_BUCKETS - 1)
    return jnp.where(n < max_exact, n, large)


def _bias_tables(rel_bias):
    assert MOBA_BLOCK + 1 >= REL_MAX_DIST
    table_t = rel_bias.T.astype(F32)
    offs = jnp.arange(MOBA_BLOCK)
    rel_own = offs[None, :] - offs[:, None]
    buckets = jnp.arange(REL_BUCKETS)[:, None, None]

    def lookup(rel):
        onehot = (_t5_bucket(rel)[None] == buckets).astype(F32)
        return jnp.einsum('hb,bkq->hkq', table_t, onehot, precision=lax.Precision.HIGHEST)

    bias0 = jnp.where(rel_own >= 0, lookup(rel_own), NEG)
    bias1 = lookup(rel_own + MOBA_BLOCK)
    far = _t5_bucket(jnp.int32(2 * MOBA_BLOCK - (MOBA_BLOCK - 1)))
    bias_far = table_t[:, far][:, None, None]
    return (bias0 - bias_far) * LOG2E, (bias1 - bias_far) * LOG2E


def _pool_lane_tiles(pool_w):
    group_w = pool_w // len(POOL_WINDOWS)
    tiles = []
    for lo in range(0, pool_w, LANES):
        windows = [w for gi, w in enumerate(POOL_WINDOWS) if lo <= gi * group_w < lo + LANES]
        lane_group = (lax.broadcasted_iota(jnp.int32, (SEQ_TILE, LANES), 1) + lo) // group_w
        window = jnp.zeros((SEQ_TILE, LANES), F32)
        for gi, w in enumerate(POOL_WINDOWS):
            window = jnp.where(lane_group == gi, float(w), window)
        tiles.append((slice(lo, lo + LANES), windows, window))
    return tiles


def _merge_kernel(x_ref, attn_ref, u_ref, gpre_ref, wg_ref, bg_ref, wa_ref, wp_ref, wc_ref, wo_ref,
                  gpost_ref, wpg_ref, ps_ref, wdw_ref, bdw_ref, lng_ref, lnb_ref, o_ref,
                  prot_ref, crot_ref, *, tiles_per_seq, pool_w, conv_w):
    d = x_ref.shape[1]
    n_rows = x_ref.shape[0]
    step = pl.program_id(0)
    tile_in_seq = step % tiles_per_seq
    seq_start = tile_in_seq == 0
    lane_tiles = _pool_lane_tiles(pool_w)
    conv_lanes = slice(0, conv_w)
    copies = [(prot_ref, lanes, min(SUBLANES, max(windows))) for lanes, windows, _ in lane_tiles]
    copies.append((crot_ref, conv_lanes, SUBLANES))

    @pl.when(step == 0)
    def _first_halo_source():
        for ref, lanes, n_copies in copies:
            for r in range(n_copies):
                ref[r, n_rows:n_rows + SEQ_PAD, lanes] = jnp.zeros((SEQ_PAD, lanes.stop - lanes.start), F32)

    for ref, lanes, n_copies in copies:
        for r in range(n_copies):
            tail = ref[r, n_rows:n_rows + SEQ_PAD, lanes]
            ref[r, 0:SEQ_PAD, lanes] = jnp.where(seq_start, 0.0, tail)

    def delayed(ref, delay, t0, lanes):
        a8, r = delay // SUBLANES * SUBLANES, delay % SUBLANES
        return ref[r, pl.ds(SEQ_PAD + t0 - a8, SEQ_TILE), lanes]

    for t0 in range(0, n_rows, SEQ_TILE):
        rows = pl.ds(t0, SEQ_TILE)
        here = pl.ds(SEQ_PAD + t0, SEQ_TILE)
        prot_ref[0, here, 0:pool_w] = u_ref[rows, 0:pool_w]
        a = u_ref[rows, pool_w:pool_w + conv_w]
        gt = u_ref[rows, pool_w + conv_w:pool_w + 2 * conv_w]
        crot_ref[0, here, conv_lanes] = a * jax.nn.sigmoid(gt)
        for ref, lanes, n_copies in copies:
            for r in range(1, n_copies):
                ref[r, here, lanes] = ref[0, pl.ds(SEQ_PAD + t0 - r, SEQ_TILE), lanes]

    row = lax.broadcasted_iota(jnp.int32, (SEQ_TILE, LANES), 0)

    def pool_branch(t0):
        parts = []
        for lanes, windows, window in lane_tiles:
            tok = delayed(prot_ref, 0, t0, lanes)
            run = tok
            wsum = jnp.zeros_like(tok)
            for k in range(1, max(windows)):
                run = run + delayed(prot_ref, k, t0, lanes)
                if k + 1 in windows:
                    wsum = jnp.where(window == float(k + 1), run, wsum)
            seen = (row + (tile_in_seq * n_rows + t0 + 1)).astype(F32)
            parts.append(wsum / jnp.minimum(seen, window) - tok)
        y = _dot(jnp.concatenate(parts, axis=1).astype(BF16), wpg_ref[...]) * ps_ref[...]
        return y.astype(BF16)

    def conv_branch(t0):
        acc = jnp.zeros((SEQ_TILE, conv_w), F32) + bdw_ref[...]
        for k in range(CONV_KERNEL):
            acc = acc + delayed(crot_ref, CONV_KERNEL - 1 - k, t0, conv_lanes) * wdw_ref[k:k + 1, :]
        mu = jnp.mean(acc, axis=-1, keepdims=True)
        cen = acc - mu
        var = jnp.mean(cen * cen, axis=-1, keepdims=True)
        yn = cen * lax.rsqrt(var + EPS) * lng_ref[...] + lnb_ref[...]
        return (yn * jax.nn.sigmoid(yn)).astype(BF16)

    groups = [slice(t0, t0 + SEQ_TILE) for t0 in range(0, n_rows, SEQ_TILE)]
    xs = [x_ref[rows, :] for rows in groups]
    hs = [_rms(x, gpre_ref[...]).astype(BF16) for x in xs]
    ys = []
    for rows, h in zip(groups, hs):
        acts = (attn_ref[rows, :], pool_branch(rows.start), conv_branch(rows.start))
        merged = None
        for i, (act, w_ref) in enumerate(zip(acts, (wa_ref, wp_ref, wc_ref))):
            cols = slice(i * d, (i + 1) * d)
            gate = jax.nn.sigmoid(_dot(h, wg_ref[0, :, cols]) + bg_ref[:, cols])
            part = gate * _dot(act, w_ref[...])
            merged = part if merged is None else merged + part
        ys.append(_dot(merged.astype(BF16), wo_ref[...]))
    for rows, x, y in zip(groups, xs, ys):
        o_ref[rows, :] = x + _rms(y, gpost_ref[...])


def _merge(x2, attn, u, seq_len, gpre, w_in, bg, wa, wp, wc, wo, gpost, wpg, ps, wdw, bdw, lng, lnb, l):
    m, d = x2.shape
    gate_w = bg.shape[2]
    pool_w, conv_w = wp.shape[1], wc.shape[1]
    tile = lambda w: pl.BlockSpec((MERGE_TILE, w), lambda i: (i, 0))
    layer = lambda a: _const_spec((None,) + a.shape[1:], (l, 0, 0))
    gate_cols = pl.BlockSpec((pl.Element(1), pl.Element(d), pl.Element(gate_w)),
                             lambda i: (l, 0, w_in.shape[2] - gate_w), pipeline_mode=pl.Buffered(1))
    kern = functools.partial(_merge_kernel, tiles_per_seq=seq_len // MERGE_TILE,
                             pool_w=pool_w, conv_w=conv_w)
    return pl.pallas_call(
        kern,
        grid=(m // MERGE_TILE,),
        in_specs=[
            tile(d), tile(attn.shape[1]), tile(u.shape[1]),
            layer(gpre), gate_cols, layer(bg), layer(wa), layer(wp), layer(wc), layer(wo),
            layer(gpost), layer(wpg), layer(ps), layer(wdw), layer(bdw), layer(lng), layer(lnb),
        ],
        out_specs=tile(d),
        out_shape=jax.ShapeDtypeStruct((m, d), F32),
        scratch_shapes=[
            pltpu.VMEM((SUBLANES, SEQ_PAD + MERGE_TILE + SEQ_PAD, pool_w), F32),
            pltpu.VMEM((SUBLANES, SEQ_PAD + MERGE_TILE + SEQ_PAD, conv_w), F32),
        ],
        compiler_params=_params("arbitrary"),
        name="merge",
    )(x2, attn, u, gpre, w_in, bg, wa, wp, wc, wo, gpost, wpg, ps, wdw, bdw, lng, lnb)


def _ffn_kernel(x_ref, gpre_ref, win_ref, wout_ref, gpost_ref, o_ref, *, d_ff):
    groups = _row_groups(x_ref.shape[0], FFN_SUBTILES)
    xs = [x_ref[rows, :] for rows in groups]
    hs = [_rms(x, gpre_ref[...]).astype(BF16) for x in xs]
    fs = []
    for h in hs:
        f = None
        for c in range(0, d_ff, FFN_CHUNK):
            w = min(FFN_CHUNK, d_ff - c)
            gt = _dot(h, win_ref[:, c:c + w])
            up = _dot(h, win_ref[:, d_ff + c:d_ff + c + w])
            act = (gt * jax.nn.sigmoid(gt) * up).astype(BF16)
            part = _dot(act, wout_ref[c:c + w, :])
            f = part if f is None else f + part
        fs.append(f)
    for rows, x, f in zip(groups, xs, fs):
        o_ref[rows, :] = x + _rms(f, gpost_ref[...])


def _ffn(x2, gpre, win, wout, gpost, l):
    m, d = x2.shape
    d_ff = wout.shape[1]
    kern = functools.partial(_ffn_kernel, d_ff=d_ff)
    return pl.pallas_call(
        kern,
        grid=(m // TOKEN_TILE,),
        in_specs=[
            pl.BlockSpec((TOKEN_TILE, d), lambda i: (i, 0)),
            _const_spec((None, 1, d), (l, 0, 0)),
            _const_spec((None, d, 2 * d_ff), (l, 0, 0)),
            _const_spec((None, d_ff, d), (l, 0, 0)),
            _const_spec((None, 1, d), (l, 0, 0)),
        ],
        out_specs=pl.BlockSpec((TOKEN_TILE, d), lambda i: (i, 0)),
        out_shape=jax.ShapeDtypeStruct((m, d), F32),
        compiler_params=_params("arbitrary"),
        name="ffn",
    )(x2, gpre, win, wout, gpost)


def _block_diag(w):
    depth, groups, gd, _ = w.shape
    eye = jnp.eye(groups, dtype=w.dtype)
    return jnp.einsum('lgcd,gh->lgchd', w, eye).reshape(depth, groups * gd, groups * gd)


def kernel(x, pre_mix_g, w_in, b_gate, w_attn_o, rel_bias, w_pool_g, pool_scale, w_pool_o, w_dw,
           b_dw, conv_ln_g, conv_ln_b, w_conv_o, w_out, post_mix_g, pre_ffn_g, w_ffn_in,
           w_ffn_out, post_ffn_g):
    b, s, d = x.shape
    depth = w_in.shape[0]
    attn_w = w_attn_o.shape[1]
    pool_w = w_pool_o.shape[1]
    conv_w = w_conv_o.shape[1]
    u_w = pool_w + 2 * conv_w

    row = lambda a: a.reshape(depth, 1, a.shape[-1])
    w_in_b = w_in.astype(BF16)
    wa_b, wp_b, wc_b, wo_b = (w.astype(BF16) for w in (w_attn_o, w_pool_o, w_conv_o, w_out))
    wfi_b, wfo_b = w_ffn_in.astype(BF16), w_ffn_out.astype(BF16)
    wg_bd = _block_diag(w_pool_g).astype(BF16)
    bias0_t, bias1_t = _bias_tables(rel_bias)

    x2 = x.reshape(b * s, d)
    for l in range(depth):
        qkv, u = _inproj(x2, row(pre_mix_g), w_in_b, l, attn_w, u_w)
        attn = _moba(qkv.reshape(b, s, 3 * attn_w), bias0_t, bias1_t, attn_w)
        x2 = _merge(x2, attn.reshape(b * s, attn_w), u, s, row(pre_mix_g), w_in_b, row(b_gate),
                    wa_b, wp_b, wc_b, wo_b, row(post_mix_g), wg_bd, row(pool_scale), w_dw,
                    row(b_dw), row(conv_ln_g), row(conv_ln_b), l)
        x2 = _ffn(x2, row(pre_ffn_g), wfi_b, wfo_b, row(post_ffn_g), l)
    return x2.reshape(b, s, d)
```

```python
import functools
import math

import jax
import jax.numpy as jnp
from jax import lax
from jax.experimental import pallas as pl
from jax.experimental.pallas import tpu as pltpu

HEAD_DIM = 64
MOBA_BLOCK = 256
MOBA_TOPK = 3
POOL_WINDOWS = (2, 4, 8, 16)
CONV_KERNEL = 31
REL_BUCKETS = 32
REL_MAX_DIST = 128
EPS = 1e-6
NEG = -1e30
LOG2E = math.log2(math.e)

LANES = 128
HEADS_PER_STEP = LANES // HEAD_DIM
TOKEN_TILE = 1024
INPROJ_SUBTILES = 2
MERGE_TILE = 512
FFN_SUBTILES = 4
FFN_CHUNK = 1024
VMEM_LIMIT = 56 * 1024 * 1024
SUBLANES = 8
SEQ_PAD = 32
SEQ_TILE = 256
ONES_ROWS = 16
MOBA_AHEAD = 3
MOBA_BUFFERS = 8

F32 = jnp.float32
BF16 = jnp.bfloat16


def _params(*sem):
    return pltpu.CompilerParams(dimension_semantics=sem, vmem_limit_bytes=VMEM_LIMIT)


def _const_spec(shape, index):
    return pl.BlockSpec(shape, lambda *_: index, pipeline_mode=pl.Buffered(1))


def _rms(x, g):
    return x * lax.rsqrt(jnp.mean(x * x, axis=-1, keepdims=True) + EPS) * g


def _dot(a, b):
    return jnp.dot(a, b, preferred_element_type=F32)


def _row_groups(n_rows, n_groups):
    size = n_rows // n_groups
    return [slice(i * size, (i + 1) * size) for i in range(n_groups)]


def _inproj_kernel(x_ref, g_ref, w_ref, qkv_ref, u_ref, h_ref, *, attn_w, conv_w, chunk):
    qkv_w = 3 * attn_w
    pool_w = u_ref.shape[1] - conv_w
    groups = _row_groups(x_ref.shape[0], INPROJ_SUBTILES)
    hs = [_rms(x_ref[rows, :], g_ref[...]).astype(BF16) for rows in groups]
    for rows, h in zip(groups, hs):
        h_ref[rows, :] = h
        for c in range(0, qkv_w, chunk):
            r = _dot(h, w_ref[:, c:c + chunk])
            if c < attn_w:
                r = r * (HEAD_DIM ** -0.5)
            qkv_ref[rows, c:c + chunk] = r.astype(BF16)
        r = _dot(h, w_ref[:, qkv_w:])
        u_ref[rows, 0:pool_w] = r[:, 0:pool_w]
        u_ref[rows, pool_w:] = r[:, pool_w:pool_w + conv_w] * jax.nn.sigmoid(r[:, pool_w + conv_w:])


def _inproj(x2, g, w, l, attn_w, u_w, conv_w):
    m, d = x2.shape
    kern = functools.partial(_inproj_kernel, attn_w=attn_w, conv_w=conv_w, chunk=512)
    return pl.pallas_call(
        kern,
        grid=(m // TOKEN_TILE,),
        in_specs=[
            pl.BlockSpec((TOKEN_TILE, d), lambda i: (i, 0)),
            _const_spec((None, 1, d), (l, 0, 0)),
            _const_spec((None, d, 3 * attn_w + u_w), (l, 0, 0)),
        ],
        out_specs=[
            pl.BlockSpec((TOKEN_TILE, 3 * attn_w), lambda i: (i, 0)),
            pl.BlockSpec((TOKEN_TILE, u_w - conv_w), lambda i: (i, 0)),
            pl.BlockSpec((TOKEN_TILE, d), lambda i: (i, 0)),
        ],
        out_shape=[
            jax.ShapeDtypeStruct((m, 3 * attn_w), BF16),
            jax.ShapeDtypeStruct((m, u_w - conv_w), F32),
            jax.ShapeDtypeStruct((m, d), BF16),
        ],
        compiler_params=_params("arbitrary"),
        name="inproj",
    )(x2, g, w)


def _moba_kernel(q_ref, k_ref, v_ref, b0_ref, b1_ref, o_ref,
                 kmean_ref, vt_ref, *bufs, n_blocks):
    blk = MOBA_BLOCK
    for j in range(n_blocks):
        kb = k_ref[0, j * blk:(j + 1) * blk, :].astype(F32)
        kmean_ref[j:j + 1, :] = jnp.mean(kb, axis=0, keepdims=True)
        vt = v_ref[0, j * blk:(j + 1) * blk, :].astype(F32).T
        for h in range(HEADS_PER_STEP):
            vt_ref[h, 0:HEAD_DIM, j * blk:(j + 1) * blk] = (
                vt[h * HEAD_DIM:(h + 1) * HEAD_DIM].astype(BF16))
    for h in range(HEADS_PER_STEP):
        vt_ref[h, HEAD_DIM:HEAD_DIM + ONES_ROWS, :] = jnp.ones((ONES_ROWS, vt_ref.shape[2]), BF16)

    dim_row = lax.broadcasted_iota(jnp.int32, (LANES, blk), 0)
    blk_row = lax.broadcasted_iota(jnp.int32, (n_blocks, blk), 0)
    dim_lane = lax.broadcasted_iota(jnp.int32, (n_blocks, LANES), 1)
    kmean_heads = jnp.concatenate(
        [jnp.where((dim_lane >= h * HEAD_DIM) & (dim_lane < (h + 1) * HEAD_DIM), kmean_ref[...], 0.0)
         for h in range(HEADS_PER_STEP)], axis=0)

    n_buf = len(bufs) // 2
    s_refs, p_refs = bufs[:n_buf], bufs[n_buf:]
    qts, gates = {}, {}

    def pass1(qi, h, s_ref):
        if qi not in qts:
            qts[qi] = q_ref[0, qi * blk:(qi + 1) * blk, :].astype(F32).T
            if qi > MOBA_TOPK:
                gates[qi] = jnp.dot(kmean_heads, qts[qi], preferred_element_type=F32,
                                    precision=lax.Precision.HIGHEST)
        in_head = (dim_row >= h * HEAD_DIM) & (dim_row < (h + 1) * HEAD_DIM)
        qt_hb = (jnp.where(in_head, qts[qi], 0.0) * LOG2E).astype(BF16)

        chosen = None
        if qi > MOBA_TOPK:
            gate = gates[qi][h * n_blocks:(h + 1) * n_blocks]
            rank = jnp.zeros((n_blocks, blk), jnp.int32)
            for j in range(qi):
                other = gate[j:j + 1, :]
                beats = (other > gate) | ((other == gate) & (j < blk_row))
                rank = rank + jnp.where(beats, 1, 0)
            sel = (blk_row < qi) & (rank < MOBA_TOPK)
            chosen = [sel[j:j + 1, :] for j in range(qi)]

        col_max = []
        n_far = max(qi - 1, 0)
        if n_far:
            r = _dot(k_ref[0, 0:n_far * blk, :], qt_hb)
            s_ref[0:n_far * blk, :] = r
            for j in range(n_far):
                col_max.append(jnp.max(r[j * blk:(j + 1) * blk], axis=0, keepdims=True))
        if qi >= 1:
            lo = (qi - 1) * blk
            r = _dot(k_ref[0, lo:lo + blk, :], qt_hb) + b1_ref[h]
            s_ref[lo:lo + blk, :] = r
            col_max.append(jnp.max(r, axis=0, keepdims=True))
        lo = qi * blk
        r = _dot(k_ref[0, lo:lo + blk, :], qt_hb) + b0_ref[h]
        s_ref[lo:lo + blk, :] = r
        m = jnp.max(r, axis=0, keepdims=True)
        if chosen is None:
            for j in range(qi):
                m = jnp.maximum(m, col_max[j])
            return [m] * (qi + 1)
        for j in range(qi):
            m = jnp.maximum(m, jnp.where(chosen[j], col_max[j], NEG))
        return [jnp.where(chosen[j], m, -NEG) for j in range(qi)] + [m]

    def pass2(qi, h, offs, s_ref, p_ref):
        for j in range(qi + 1):
            x = s_ref[j * blk:(j + 1) * blk, :] - offs[j]
            p_ref[j * blk:(j + 1) * blk, :] = jnp.exp2(x.astype(BF16))
        n_keys = (qi + 1) * blk
        acc = _dot(vt_ref[h, :, 0:n_keys], p_ref[0:n_keys, :])
        return acc[0:HEAD_DIM] / acc[HEAD_DIM:HEAD_DIM + 1]

    units = [(qi, h) for qi in reversed(range(n_blocks)) for h in range(HEADS_PER_STEP)]
    offs, outs = {}, {}
    for n in range(len(units) + MOBA_AHEAD):
        if n < len(units):
            offs[n] = pass1(*units[n], s_refs[n % n_buf])
        if n >= MOBA_AHEAD:
            d = n - MOBA_AHEAD
            qi, h = units[d]
            outs[h] = pass2(qi, h, offs.pop(d), s_refs[d % n_buf], p_refs[d % n_buf])
            if h == HEADS_PER_STEP - 1:
                both = jnp.concatenate([outs[i] for i in range(HEADS_PER_STEP)], axis=0)
                o_ref[0, qi * blk:(qi + 1) * blk, :] = both.T.astype(o_ref.dtype)


def _moba(qkv3, bias0_t, bias1_t, attn_w):
    b, s, _ = qkv3.shape
    n_heads = attn_w // HEAD_DIM
    n_blocks = s // MOBA_BLOCK
    hp = n_heads // HEADS_PER_STEP
    cols = attn_w // LANES
    n_buf = MOBA_BUFFERS
    kern = functools.partial(_moba_kernel, n_blocks=n_blocks)
    bias_spec = pl.BlockSpec((HEADS_PER_STEP, MOBA_BLOCK, MOBA_BLOCK), lambda p, i: (p, 0, 0))
    return pl.pallas_call(
        kern,
        grid=(hp, b),
        in_specs=[
            pl.BlockSpec((1, s, LANES), lambda p, i: (i, 0, p)),
            pl.BlockSpec((1, s, LANES), lambda p, i: (i, 0, cols + p)),
            pl.BlockSpec((1, s, LANES), lambda p, i: (i, 0, 2 * cols + p)),
            bias_spec,
            bias_spec,
        ],
        out_specs=pl.BlockSpec((1, s, LANES), lambda p, i: (i, 0, p)),
        out_shape=jax.ShapeDtypeStruct((b, s, attn_w), BF16),
        scratch_shapes=[
            pltpu.VMEM((n_blocks, LANES), F32),
            pltpu.VMEM((HEADS_PER_STEP, HEAD_DIM + ONES_ROWS, s), BF16),
        ] + [pltpu.VMEM((s, MOBA_BLOCK), F32)] * n_buf + [pltpu.VMEM((s, MOBA_BLOCK), BF16)] * n_buf,
        compiler_params=_params("arbitrary", "arbitrary"),
        name="moba",
    )(qkv3, qkv3, qkv3, bias0_t, bias1_t)


def _t5_bucket(rel):
    n = jnp.maximum(rel, 0)
    max_exact = REL_BUCKETS // 2
    nf = jnp.maximum(n, 1).astype(F32)
    large = max_exact + (jnp.log(nf / max_exact) / math.log(REL_MAX_DIST / max_exact)
                         * (REL_BUCKETS - max_exact)).astype(jnp.int32)
    large = jnp.minimum(large, REL_BUCKETS - 1)
    return jnp.where(n < max_exact, n, large)


def _bias_tables(rel_bias):
    assert MOBA_BLOCK + 1 >= REL_MAX_DIST
    table_t = rel_bias.T.astype(F32)
    offs = jnp.arange(MOBA_BLOCK)
    rel_own = offs[None, :] - offs[:, None]
    buckets = jnp.arange(REL_BUCKETS)[:, None, None]

    def lookup(rel):
        onehot = (_t5_bucket(rel)[None] == buckets).astype(F32)
        return jnp.einsum('hb,bkq->hkq', table_t, onehot, precision=lax.Precision.HIGHEST)

    bias0 = jnp.where(rel_own >= 0, lookup(rel_own), NEG)
    bias1 = lookup(rel_own + MOBA_BLOCK)
    far = _t5_bucket(jnp.int32(2 * MOBA_BLOCK - (MOBA_BLOCK - 1)))
    bias_far = table_t[:, far][:, None, None]
    return (bias0 - bias_far) * LOG2E, (bias1 - bias_far) * LOG2E


def _pool_lane_tiles(pool_w):
    group_w = pool_w // len(POOL_WINDOWS)
    tiles = []
    for lo in range(0, pool_w, LANES):
        windows = [w for gi, w in enumerate(POOL_WINDOWS) if lo <= gi * group_w < lo + LANES]
        lane_group = (lax.broadcasted_iota(jnp.int32, (SEQ_TILE, LANES), 1) + lo) // group_w
        window = jnp.zeros((SEQ_TILE, LANES), F32)
        for gi, w in enumerate(POOL_WINDOWS):
            window = jnp.where(lane_group == gi, float(w), window)
        tiles.append((slice(lo, lo + LANES), windows, window))
    return tiles


def _merge_kernel(x_ref, attn_ref, u_ref, h_ref, wg_ref, bg_ref, wa_ref, wp_ref, wc_ref, wo_ref,
                  gpost_ref, wpg_ref, ps_ref, wdw_ref, bdw_ref, lng_ref, lnb_ref, o_ref,
                  prot_ref, crot_ref, *, tiles_per_seq, pool_w, conv_w):
    d = x_ref.shape[1]
    n_rows = x_ref.shape[0]
    step = pl.program_id(0)
    tile_in_seq = step % tiles_per_seq
    seq_start = tile_in_seq == 0
    lane_tiles = _pool_lane_tiles(pool_w)
    conv_lanes = slice(0, conv_w)
    copies = [(prot_ref, lanes, min(SUBLANES, max(windows))) for lanes, windows, _ in lane_tiles]
    copies.append((crot_ref, conv_lanes, SUBLANES))

    @pl.when(step == 0)
    def _first_halo_source():
        for ref, lanes, n_copies in copies:
            for r in range(n_copies):
                ref[r, n_rows:n_rows + SEQ_PAD, lanes] = jnp.zeros((SEQ_PAD, lanes.stop - lanes.start), F32)

    for ref, lanes, n_copies in copies:
        for r in range(n_copies):
            tail = ref[r, n_rows:n_rows + SEQ_PAD, lanes]
            ref[r, 0:SEQ_PAD, lanes] = jnp.where(seq_start, 0.0, tail)

    def delayed(ref, delay, t0, lanes):
        a8, r = delay // SUBLANES * SUBLANES, delay % SUBLANES
        return ref[r, pl.ds(SEQ_PAD + t0 - a8, SEQ_TILE), lanes]

    for t0 in range(0, n_rows, SEQ_TILE):
        rows = pl.ds(t0, SEQ_TILE)
        here = pl.ds(SEQ_PAD + t0, SEQ_TILE)
        prot_ref[0, here, 0:pool_w] = u_ref[rows, 0:pool_w]
        crot_ref[0, here, conv_lanes] = u_ref[rows, pool_w:pool_w + conv_w]
        for ref, lanes, n_copies in copies:
            for r in range(1, n_copies):
                ref[r, here, lanes] = ref[0, pl.ds(SEQ_PAD + t0 - r, SEQ_TILE), lanes]

    row = lax.broadcasted_iota(jnp.int32, (SEQ_TILE, LANES), 0)

    def pool_branch(t0):
        parts = []
        for lanes, windows, window in lane_tiles:
            tok = delayed(prot_ref, 0, t0, lanes)
            run = tok
            wsum = jnp.zeros_like(tok)
            for k in range(1, max(windows)):
                run = run + delayed(prot_ref, k, t0, lanes)
                if k + 1 in windows:
                    wsum = jnp.where(window == float(k + 1), run, wsum)
            seen = (row + (tile_in_seq * n_rows + t0 + 1)).astype(F32)
            parts.append(wsum / jnp.minimum(seen, window) - tok)
        y = _dot(jnp.concatenate(parts, axis=1).astype(BF16), wpg_ref[...]) * ps_ref[...]
        return y.astype(BF16)

    def conv_branch(t0):
        acc = jnp.zeros((SEQ_TILE, conv_w), F32) + bdw_ref[...]
        for k in range(CONV_KERNEL):
            acc = acc + delayed(crot_ref, CONV_KERNEL - 1 - k, t0, conv_lanes) * wdw_ref[k:k + 1, :]
        mu = jnp.mean(acc, axis=-1, keepdims=True)
        cen = acc - mu
        var = jnp.mean(cen * cen, axis=-1, keepdims=True)
        yn = cen * lax.rsqrt(var + EPS) * lng_ref[...] + lnb_ref[...]
        return (yn * jax.nn.sigmoid(yn)).astype(BF16)

    groups = [slice(t0, t0 + SEQ_TILE) for t0 in range(0, n_rows, SEQ_TILE)]
    ys = []
    for rows in groups:
        h = h_ref[rows, :]
        acts = (attn_ref[rows, :], pool_branch(rows.start), conv_branch(rows.start))
        merged = None
        for i, (act, w_ref) in enumerate(zip(acts, (wa_ref, wp_ref, wc_ref))):
            cols = slice(i * d, (i + 1) * d)
            gate = jax.nn.sigmoid(_dot(h, wg_ref[0, :, cols]) + bg_ref[:, cols])
            part = gate * _dot(act, w_ref[...])
            merged = part if merged is None else merged + part
        ys.append(_dot(merged.astype(BF16), wo_ref[...]))
    for rows, y in zip(groups, ys):
        o_ref[rows, :] = x_ref[rows, :] + _rms(y, gpost_ref[...])


def _merge(x2, attn, u, hn, seq_len, w_in, bg, wa, wp, wc, wo, gpost, wpg, ps, wdw, bdw, lng, lnb, l):
    m, d = x2.shape
    gate_w = bg.shape[2]
    pool_w, conv_w = wp.shape[1], wc.shape[1]
    tile = lambda w: pl.BlockSpec((MERGE_TILE, w), lambda i: (i, 0))
    layer = lambda a: _const_spec((None,) + a.shape[1:], (l, 0, 0))
    gate_cols = pl.BlockSpec((pl.Element(1), pl.Element(d), pl.Element(gate_w)),
                             lambda i: (l, 0, w_in.shape[2] - gate_w), pipeline_mode=pl.Buffered(1))
    kern = functools.partial(_merge_kernel, tiles_per_seq=seq_len // MERGE_TILE,
                             pool_w=pool_w, conv_w=conv_w)
    return pl.pallas_call(
        kern,
        grid=(m // MERGE_TILE,),
        in_specs=[
            tile(d), tile(attn.shape[1]), tile(u.shape[1]), tile(d), gate_cols, layer(bg), layer(wa), layer(wp), layer(wc), layer(wo),
            layer(gpost), layer(wpg), layer(ps), layer(wdw), layer(bdw), layer(lng), layer(lnb),
        ],
        out_specs=tile(d),
        out_shape=jax.ShapeDtypeStruct((m, d), F32),
        scratch_shapes=[
            pltpu.VMEM((SUBLANES, SEQ_PAD + MERGE_TILE + SEQ_PAD, pool_w), F32),
            pltpu.VMEM((SUBLANES, SEQ_PAD + MERGE_TILE + SEQ_PAD, conv_w), F32),
        ],
        compiler_params=_params("arbitrary"),
        name="merge",
    )(x2, attn, u, hn, w_in, bg, wa, wp, wc, wo, gpost, wpg, ps, wdw, bdw, lng, lnb)


def _ffn_kernel(x_ref, gpre_ref, win_ref, wout_ref, gpost_ref, o_ref, *, d_ff):
    groups = _row_groups(x_ref.shape[0], FFN_SUBTILES)
    xs = [x_ref[rows, :] for rows in groups]
    hs = [_rms(x, gpre_ref[...]).astype(BF16) for x in xs]
    fs = []
    for h in hs:
        f = None
        for c in range(0, d_ff, FFN_CHUNK):
            w = min(FFN_CHUNK, d_ff - c)
            gt = _dot(h, win_ref[:, c:c + w])
            up = _dot(h, win_ref[:, d_ff + c:d_ff + c + w])
            act = (gt * jax.nn.sigmoid(gt) * up).astype(BF16)
            part = _dot(act, wout_ref[c:c + w, :])
            f = part if f is None else f + part
        fs.append(f)
    for rows, x, f in zip(groups, xs, fs):
        o_ref[rows, :] = x + _rms(f, gpost_ref[...])


def _ffn(x2, gpre, win, wout, gpost, l):
    m, d = x2.shape
    d_ff = wout.shape[1]
    kern = functools.partial(_ffn_kernel, d_ff=d_ff)
    return pl.pallas_call(
        kern,
        grid=(m // TOKEN_TILE,),
        in_specs=[
            pl.BlockSpec((TOKEN_TILE, d), lambda i: (i, 0)),
            _const_spec((None, 1, d), (l, 0, 0)),
            _const_spec((None, d, 2 * d_ff), (l, 0, 0)),
            _const_spec((None, d_ff, d), (l, 0, 0)),
            _const_spec((None, 1, d), (l, 0, 0)),
        ],
        out_specs=pl.BlockSpec((TOKEN_TILE, d), lambda i: (i, 0)),
        out_shape=jax.ShapeDtypeStruct((m, d), F32),
        compiler_params=_params("arbitrary"),
        name="ffn",
    )(x2, gpre, win, wout, gpost)


def _block_diag(w):
    depth, groups, gd, _ = w.shape
    eye = jnp.eye(groups, dtype=w.dtype)
    return jnp.einsum('lgcd,gh->lgchd', w, eye).reshape(depth, groups * gd, groups * gd)


def kernel(x, pre_mix_g, w_in, b_gate, w_attn_o, rel_bias, w_pool_g, pool_scale, w_pool_o, w_dw,
           b_dw, conv_ln_g, conv_ln_b, w_conv_o, w_out, post_mix_g, pre_ffn_g, w_ffn_in,
           w_ffn_out, post_ffn_g):
    b, s, d = x.shape
    depth = w_in.shape[0]
    attn_w = w_attn_o.shape[1]
    pool_w = w_pool_o.shape[1]
    conv_w = w_conv_o.shape[1]
    u_w = pool_w + 2 * conv_w

    row = lambda a: a.reshape(depth, 1, a.shape[-1])
    w_in_b = w_in.astype(BF16)
    wa_b, wp_b, wc_b, wo_b = (w.astype(BF16) for w in (w_attn_o, w_pool_o, w_conv_o, w_out))
    wfi_b, wfo_b = w_ffn_in.astype(BF16), w_ffn_out.astype(BF16)
    wg_bd = _block_diag(w_pool_g).astype(BF16)
    bias0_t, bias1_t = _bias_tables(rel_bias)

    x2 = x.reshape(b * s, d)
    for l in range(depth):
        qkv, u, hn = _inproj(x2, row(pre_mix_g), w_in_b, l, attn_w, u_w, conv_w)
        attn = _moba(qkv.reshape(b, s, 3 * attn_w), bias0_t, bias1_t, attn_w)
        x2 = _merge(x2, attn.reshape(b * s, attn_w), u, hn, s, w_in_b, row(b_gate),
                    wa_b, wp_b, wc_b, wo_b, row(post_mix_g), wg_bd, row(pool_scale), w_dw,
                    row(b_dw), row(conv_ln_g), row(conv_ln_b), l)
        x2 = _ffn(x2, row(pre_ffn_g), wfi_b, wfo_b, row(post_ffn_g), l)
    return x2.reshape(b, s, d)
```

```python
import functools
import math

import jax
import jax.numpy as jnp
from jax import lax
from jax.experimental import pallas as pl
from jax.experimental.pallas import tpu as pltpu

HEAD_DIM = 64
MOBA_BLOCK = 256
MOBA_TOPK = 3
POOL_WINDOWS = (2, 4, 8, 16)
CONV_KERNEL = 31
REL_BUCKETS = 32
REL_MAX_DIST = 128
EPS = 1e-6
NEG = -1e30
LOG2E = math.log2(math.e)

LANES = 128
HEADS_PER_STEP = LANES // HEAD_DIM
TOKEN_TILE = 1024
INPROJ_SUBTILES = 2
MERGE_TILE = 512
FFN_SUBTILES = 4
FFN_CHUNK = 1024
VMEM_LIMIT = 56 * 1024 * 1024
SUBLANES = 8
SEQ_PAD = 32
SEQ_TILE = 256
ONES_ROWS = 16
MOBA_AHEAD = 3
MOBA_BUFFERS = 8

F32 = jnp.float32
BF16 = jnp.bfloat16


def _params(*sem):
    return pltpu.CompilerParams(dimension_semantics=sem, vmem_limit_bytes=VMEM_LIMIT)


def _const_spec(shape, index):
    return pl.BlockSpec(shape, lambda *_: index, pipeline_mode=pl.Buffered(1))


def _rms(x, g):
    return x * lax.rsqrt(jnp.mean(x * x, axis=-1, keepdims=True) + EPS) * g


def _dot(a, b):
    return jnp.dot(a, b, preferred_element_type=F32)


def _row_groups(n_rows, n_groups):
    size = n_rows // n_groups
    return [slice(i * size, (i + 1) * size) for i in range(n_groups)]


def _inproj_kernel(x_ref, g_ref, w_ref, qkv_ref, u_ref, h_ref, *, attn_w, conv_w, chunk):
    qkv_w = 3 * attn_w
    pool_w = u_ref.shape[1] - conv_w
    groups = _row_groups(x_ref.shape[0], INPROJ_SUBTILES)
    hs = [_rms(x_ref[rows, :], g_ref[...]).astype(BF16) for rows in groups]
    for rows, h in zip(groups, hs):
        h_ref[rows, :] = h
        for c in range(0, qkv_w, chunk):
            r = _dot(h, w_ref[:, c:c + chunk])
            if c < attn_w:
                r = r * (HEAD_DIM ** -0.5 * LOG2E)
            qkv_ref[rows, c:c + chunk] = r.astype(BF16)
        r = _dot(h, w_ref[:, qkv_w:])
        u_ref[rows, 0:pool_w] = r[:, 0:pool_w]
        u_ref[rows, pool_w:] = r[:, pool_w:pool_w + conv_w] * jax.nn.sigmoid(r[:, pool_w + conv_w:])


def _inproj(x2, g, w, l, attn_w, u_w, conv_w):
    m, d = x2.shape
    kern = functools.partial(_inproj_kernel, attn_w=attn_w, conv_w=conv_w, chunk=512)
    return pl.pallas_call(
        kern,
        grid=(m // TOKEN_TILE,),
        in_specs=[
            pl.BlockSpec((TOKEN_TILE, d), lambda i: (i, 0)),
            _const_spec((None, 1, d), (l, 0, 0)),
            _const_spec((None, d, 3 * attn_w + u_w), (l, 0, 0)),
        ],
        out_specs=[
            pl.BlockSpec((TOKEN_TILE, 3 * attn_w), lambda i: (i, 0)),
            pl.BlockSpec((TOKEN_TILE, u_w - conv_w), lambda i: (i, 0)),
            pl.BlockSpec((TOKEN_TILE, d), lambda i: (i, 0)),
        ],
        out_shape=[
            jax.ShapeDtypeStruct((m, 3 * attn_w), BF16),
            jax.ShapeDtypeStruct((m, u_w - conv_w), F32),
            jax.ShapeDtypeStruct((m, d), BF16),
        ],
        compiler_params=_params("arbitrary"),
        name="inproj",
    )(x2, g, w)


def _moba_kernel(q_ref, k_ref, v_ref, b0_ref, b1_ref, o_ref,
                 kmean_ref, vt_ref, *bufs, n_blocks):
    blk = MOBA_BLOCK
    for j in range(n_blocks):
        kb = k_ref[0, j * blk:(j + 1) * blk, :].astype(F32)
        kmean_ref[j:j + 1, :] = jnp.mean(kb, axis=0, keepdims=True)
        vt = v_ref[0, j * blk:(j + 1) * blk, :].astype(F32).T
        for h in range(HEADS_PER_STEP):
            vt_ref[h, 0:HEAD_DIM, j * blk:(j + 1) * blk] = (
                vt[h * HEAD_DIM:(h + 1) * HEAD_DIM].astype(BF16))
    for h in range(HEADS_PER_STEP):
        vt_ref[h, HEAD_DIM:HEAD_DIM + ONES_ROWS, :] = jnp.ones((ONES_ROWS, vt_ref.shape[2]), BF16)

    dim_row = lax.broadcasted_iota(jnp.int32, (LANES, blk), 0)
    blk_row = lax.broadcasted_iota(jnp.int32, (n_blocks, blk), 0)
    dim_lane = lax.broadcasted_iota(jnp.int32, (n_blocks, LANES), 1)
    kmean_heads = jnp.concatenate(
        [jnp.where((dim_lane >= h * HEAD_DIM) & (dim_lane < (h + 1) * HEAD_DIM), kmean_ref[...], 0.0)
         for h in range(HEADS_PER_STEP)], axis=0)

    n_buf = len(bufs) // 2
    s_refs, p_refs = bufs[:n_buf], bufs[n_buf:]
    qts, gates = {}, {}

    def pass1(qi, h, s_ref):
        if qi not in qts:
            qts[qi] = q_ref[0, qi * blk:(qi + 1) * blk, :].astype(F32).T
            if qi > MOBA_TOPK:
                gates[qi] = jnp.dot(kmean_heads, qts[qi], preferred_element_type=F32,
                                    precision=lax.Precision.HIGHEST)
        in_head = (dim_row >= h * HEAD_DIM) & (dim_row < (h + 1) * HEAD_DIM)
        qt_hb = jnp.where(in_head, qts[qi], 0.0).astype(BF16)

        chosen = None
        if qi > MOBA_TOPK:
            gate = gates[qi][h * n_blocks:(h + 1) * n_blocks]
            rank = jnp.zeros((n_blocks, blk), jnp.int32)
            for j in range(qi):
                other = gate[j:j + 1, :]
                beats = (other > gate) | ((other == gate) & (j < blk_row))
                rank = rank + jnp.where(beats, 1, 0)
            sel = (blk_row < qi) & (rank < MOBA_TOPK)
            chosen = [sel[j:j + 1, :] for j in range(qi)]

        col_max = []
        n_far = max(qi - 1, 0)
        if n_far:
            r = _dot(k_ref[0, 0:n_far * blk, :], qt_hb)
            s_ref[0:n_far * blk, :] = r
            for j in range(n_far):
                col_max.append(jnp.max(r[j * blk:(j + 1) * blk], axis=0, keepdims=True))
        if qi >= 1:
            lo = (qi - 1) * blk
            r = _dot(k_ref[0, lo:lo + blk, :], qt_hb) + b1_ref[h]
            s_ref[lo:lo + blk, :] = r
            col_max.append(jnp.max(r, axis=0, keepdims=True))
        lo = qi * blk
        r = _dot(k_ref[0, lo:lo + blk, :], qt_hb) + b0_ref[h]
        s_ref[lo:lo + blk, :] = r
        m = jnp.max(r, axis=0, keepdims=True)
        if chosen is None:
            for j in range(qi):
                m = jnp.maximum(m, col_max[j])
            return [m] * (qi + 1)
        for j in range(qi):
            m = jnp.maximum(m, jnp.where(chosen[j], col_max[j], NEG))
        return [jnp.where(chosen[j], m, -NEG) for j in range(qi)] + [m]

    def pass2(qi, h, offs, s_ref, p_ref):
        for j in range(qi + 1):
            x = s_ref[j * blk:(j + 1) * blk, :] - offs[j]
            p_ref[j * blk:(j + 1) * blk, :] = jnp.exp2(x.astype(BF16))
        n_keys = (qi + 1) * blk
        acc = _dot(vt_ref[h, :, 0:n_keys], p_ref[0:n_keys, :])
        return acc[0:HEAD_DIM] / acc[HEAD_DIM:HEAD_DIM + 1]

    units = [(qi, h) for qi in reversed(range(n_blocks)) for h in range(HEADS_PER_STEP)]
    offs, outs = {}, {}
    for n in range(len(units) + MOBA_AHEAD):
        if n < len(units):
            offs[n] = pass1(*units[n], s_refs[n % n_buf])
        if n >= MOBA_AHEAD:
            d = n - MOBA_AHEAD
            qi, h = units[d]
            outs[h] = pass2(qi, h, offs.pop(d), s_refs[d % n_buf], p_refs[d % n_buf])
            if h == HEADS_PER_STEP - 1:
                both = jnp.concatenate([outs[i] for i in range(HEADS_PER_STEP)], axis=0)
                o_ref[0, qi * blk:(qi + 1) * blk, :] = both.T.astype(o_ref.dtype)


def _moba(qkv3, bias0_t, bias1_t, attn_w):
    b, s, _ = qkv3.shape
    n_heads = attn_w // HEAD_DIM
    n_blocks = s // MOBA_BLOCK
    hp = n_heads // HEADS_PER_STEP
    cols = attn_w // LANES
    n_buf = MOBA_BUFFERS
    kern = functools.partial(_moba_kernel, n_blocks=n_blocks)
    bias_spec = pl.BlockSpec((HEADS_PER_STEP, MOBA_BLOCK, MOBA_BLOCK), lambda p, i: (p, 0, 0))
    return pl.pallas_call(
        kern,
        grid=(hp, b),
        in_specs=[
            pl.BlockSpec((1, s, LANES), lambda p, i: (i, 0, p)),
            pl.BlockSpec((1, s, LANES), lambda p, i: (i, 0, cols + p)),
            pl.BlockSpec((1, s, LANES), lambda p, i: (i, 0, 2 * cols + p)),
            bias_spec,
            bias_spec,
        ],
        out_specs=pl.BlockSpec((1, s, LANES), lambda p, i: (i, 0, p)),
        out_shape=jax.ShapeDtypeStruct((b, s, attn_w), BF16),
        scratch_shapes=[
            pltpu.VMEM((n_blocks, LANES), F32),
            pltpu.VMEM((HEADS_PER_STEP, HEAD_DIM + ONES_ROWS, s), BF16),
        ] + [pltpu.VMEM((s, MOBA_BLOCK), F32)] * n_buf + [pltpu.VMEM((s, MOBA_BLOCK), BF16)] * n_buf,
        compiler_params=_params("arbitrary", "arbitrary"),
        name="moba",
    )(qkv3, qkv3, qkv3, bias0_t, bias1_t)


def _t5_bucket(rel):
    n = jnp.maximum(rel, 0)
    max_exact = REL_BUCKETS // 2
    nf = jnp.maximum(n, 1).astype(F32)
    large = max_exact + (jnp.log(nf / max_exact) / math.log(REL_MAX_DIST / max_exact)
                         * (REL_BUCKETS - max_exact)).astype(jnp.int32)
    large = jnp.minimum(large, REL_BUCKETS - 1)
    return jnp.where(n < max_exact, n, large)


def _bias_tables(rel_bias):
    assert MOBA_BLOCK + 1 >= REL_MAX_DIST
    table_t = rel_bias.T.astype(F32)
    offs = jnp.arange(MOBA_BLOCK)
    rel_own = offs[None, :] - offs[:, None]
    buckets = jnp.arange(REL_BUCKETS)[:, None, None]

    def lookup(rel):
        onehot = (_t5_bucket(rel)[None] == buckets).astype(F32)
        return jnp.einsum('hb,bkq->hkq', table_t, onehot, precision=lax.Precision.HIGHEST)

    bias0 = jnp.where(rel_own >= 0, lookup(rel_own), NEG)
    bias1 = lookup(rel_own + MOBA_BLOCK)
    far = _t5_bucket(jnp.int32(2 * MOBA_BLOCK - (MOBA_BLOCK - 1)))
    bias_far = table_t[:, far][:, None, None]
    return (bias0 - bias_far) * LOG2E, (bias1 - bias_far) * LOG2E


def _pool_lane_tiles(pool_w):
    group_w = pool_w // len(POOL_WINDOWS)
    tiles = []
    for lo in range(0, pool_w, LANES):
        windows = [w for gi, w in enumerate(POOL_WINDOWS) if lo <= gi * group_w < lo + LANES]
        lane_group = (lax.broadcasted_iota(jnp.int32, (SEQ_TILE, LANES), 1) + lo) // group_w
        window = jnp.zeros((SEQ_TILE, LANES), F32)
        for gi, w in enumerate(POOL_WINDOWS):
            window = jnp.where(lane_group == gi, float(w), window)
        tiles.append((slice(lo, lo + LANES), windows, window))
    return tiles


def _merge_kernel(x_ref, attn_ref, u_ref, h_ref, wg_ref, bg_ref, wa_ref, wp_ref, wc_ref, wo_ref,
                  gpost_ref, wpg_ref, ps_ref, wdw_ref, bdw_ref, lng_ref, lnb_ref, o_ref,
                  prot_ref, crot_ref, *, tiles_per_seq, pool_w, conv_w):
    d = x_ref.shape[1]
    n_rows = x_ref.shape[0]
    step = pl.program_id(0)
    tile_in_seq = step % tiles_per_seq
    seq_start = tile_in_seq == 0
    lane_tiles = _pool_lane_tiles(pool_w)
    conv_lanes = slice(0, conv_w)
    copies = [(prot_ref, lanes, min(SUBLANES, max(windows))) for lanes, windows, _ in lane_tiles]
    copies.append((crot_ref, conv_lanes, SUBLANES))

    @pl.when(step == 0)
    def _first_halo_source():
        for ref, lanes, n_copies in copies:
            for r in range(n_copies):
                ref[r, n_rows:n_rows + SEQ_PAD, lanes] = jnp.zeros((SEQ_PAD, lanes.stop - lanes.start), F32)

    for ref, lanes, n_copies in copies:
        for r in range(n_copies):
            tail = ref[r, n_rows:n_rows + SEQ_PAD, lanes]
            ref[r, 0:SEQ_PAD, lanes] = jnp.where(seq_start, 0.0, tail)

    def delayed(ref, delay, t0, lanes):
        a8, r = delay // SUBLANES * SUBLANES, delay % SUBLANES
        return ref[r, pl.ds(SEQ_PAD + t0 - a8, SEQ_TILE), lanes]

    for t0 in range(0, n_rows, SEQ_TILE):
        rows = pl.ds(t0, SEQ_TILE)
        here = pl.ds(SEQ_PAD + t0, SEQ_TILE)
        prot_ref[0, here, 0:pool_w] = u_ref[rows, 0:pool_w]
        crot_ref[0, here, conv_lanes] = u_ref[rows, pool_w:pool_w + conv_w]
        for ref, lanes, n_copies in copies:
            for r in range(1, n_copies):
                ref[r, here, lanes] = ref[0, pl.ds(SEQ_PAD + t0 - r, SEQ_TILE), lanes]

    row = lax.broadcasted_iota(jnp.int32, (SEQ_TILE, LANES), 0)

    def pool_branch(t0):
        parts = []
        for lanes, windows, window in lane_tiles:
            tok = delayed(prot_ref, 0, t0, lanes)
            run = tok
            wsum = jnp.zeros_like(tok)
            for k in range(1, max(windows)):
                run = run + delayed(prot_ref, k, t0, lanes)
                if k + 1 in windows:
                    wsum = jnp.where(window == float(k + 1), run, wsum)
            seen = (row + (tile_in_seq * n_rows + t0 + 1)).astype(F32)
            parts.append(wsum / jnp.minimum(seen, window) - tok)
        y = _dot(jnp.concatenate(parts, axis=1).astype(BF16), wpg_ref[...]) * ps_ref[...]
        return y.astype(BF16)

    def conv_branch(t0):
        acc = jnp.zeros((SEQ_TILE, conv_w), F32) + bdw_ref[...]
        for k in range(CONV_KERNEL):
            acc = acc + delayed(crot_ref, CONV_KERNEL - 1 - k, t0, conv_lanes) * wdw_ref[k:k + 1, :]
        mu = jnp.mean(acc, axis=-1, keepdims=True)
        cen = acc - mu
        var = jnp.mean(cen * cen, axis=-1, keepdims=True)
        yn = cen * lax.rsqrt(var + EPS) * lng_ref[...] + lnb_ref[...]
        return (yn * jax.nn.sigmoid(yn)).astype(BF16)

    groups = [slice(t0, t0 + SEQ_TILE) for t0 in range(0, n_rows, SEQ_TILE)]
    ys = []
    for rows in groups:
        h = h_ref[rows, :]
        acts = (attn_ref[rows, :], pool_branch(rows.start), conv_branch(rows.start))
        merged = None
        for i, (act, w_ref) in enumerate(zip(acts, (wa_ref, wp_ref, wc_ref))):
            cols = slice(i * d, (i + 1) * d)
            gate = jax.nn.sigmoid(_dot(h, wg_ref[0, :, cols]) + bg_ref[:, cols])
            part = gate * _dot(act, w_ref[...])
            merged = part if merged is None else merged + part
        ys.append(_dot(merged.astype(BF16), wo_ref[...]))
    for rows, y in zip(groups, ys):
        o_ref[rows, :] = x_ref[rows, :] + _rms(y, gpost_ref[...])


def _merge(x2, attn, u, hn, seq_len, w_in, bg, wa, wp, wc, wo, gpost, wpg, ps, wdw, bdw, lng, lnb, l):
    m, d = x2.shape
    gate_w = bg.shape[2]
    pool_w, conv_w = wp.shape[1], wc.shape[1]
    tile = lambda w: pl.BlockSpec((MERGE_TILE, w), lambda i: (i, 0))
    layer = lambda a: _const_spec((None,) + a.shape[1:], (l, 0, 0))
    gate_cols = pl.BlockSpec((pl.Element(1), pl.Element(d), pl.Element(gate_w)),
                             lambda i: (l, 0, w_in.shape[2] - gate_w), pipeline_mode=pl.Buffered(1))
    kern = functools.partial(_merge_kernel, tiles_per_seq=seq_len // MERGE_TILE,
                             pool_w=pool_w, conv_w=conv_w)
    return pl.pallas_call(
        kern,
        grid=(m // MERGE_TILE,),
        in_specs=[
            tile(d), tile(attn.shape[1]), tile(u.shape[1]), tile(d), gate_cols, layer(bg), layer(wa), layer(wp), layer(wc), layer(wo),
            layer(gpost), layer(wpg), layer(ps), layer(wdw), layer(bdw), layer(lng), layer(lnb),
        ],
        out_specs=tile(d),
        out_shape=jax.ShapeDtypeStruct((m, d), F32),
        scratch_shapes=[
            pltpu.VMEM((SUBLANES, SEQ_PAD + MERGE_TILE + SEQ_PAD, pool_w), F32),
            pltpu.VMEM((SUBLANES, SEQ_PAD + MERGE_TILE + SEQ_PAD, conv_w), F32),
        ],
        compiler_params=_params("arbitrary"),
        name="merge",
    )(x2, attn, u, hn, w_in, bg, wa, wp, wc, wo, gpost, wpg, ps, wdw, bdw, lng, lnb)


def _ffn_kernel(x_ref, gpre_ref, win_ref, wout_ref, gpost_ref, o_ref, *, d_ff):
    groups = _row_groups(x_ref.shape[0], FFN_SUBTILES)
    xs = [x_ref[rows, :] for rows in groups]
    hs = [_rms(x, gpre_ref[...]).astype(BF16) for x in xs]
    fs = []
    for h in hs:
        f = None
        for c in range(0, d_ff, FFN_CHUNK):
            w = min(FFN_CHUNK, d_ff - c)
            gt = _dot(h, win_ref[:, c:c + w])
            up = _dot(h, win_ref[:, d_ff + c:d_ff + c + w])
            act = (gt * jax.nn.sigmoid(gt) * up).astype(BF16)
            part = _dot(act, wout_ref[c:c + w, :])
            f = part if f is None else f + part
        fs.append(f)
    for rows, x, f in zip(groups, xs, fs):
        o_ref[rows, :] = x + _rms(f, gpost_ref[...])


def _ffn(x2, gpre, win, wout, gpost, l):
    m, d = x2.shape
    d_ff = wout.shape[1]
    kern = functools.partial(_ffn_kernel, d_ff=d_ff)
    return pl.pallas_call(
        kern,
        grid=(m // TOKEN_TILE,),
        in_specs=[
            pl.BlockSpec((TOKEN_TILE, d), lambda i: (i, 0)),
            _const_spec((None, 1, d), (l, 0, 0)),
            _const_spec((None, d, 2 * d_ff), (l, 0, 0)),
            _const_spec((None, d_ff, d), (l, 0, 0)),
            _const_spec((None, 1, d), (l, 0, 0)),
        ],
        out_specs=pl.BlockSpec((TOKEN_TILE, d), lambda i: (i, 0)),
        out_shape=jax.ShapeDtypeStruct((m, d), F32),
        compiler_params=_params("arbitrary"),
        name="ffn",
    )(x2, gpre, win, wout, gpost)


def _block_diag(w):
    depth, groups, gd, _ = w.shape
    eye = jnp.eye(groups, dtype=w.dtype)
    return jnp.einsum('lgcd,gh->lgchd', w, eye).reshape(depth, groups * gd, groups * gd)


def kernel(x, pre_mix_g, w_in, b_gate, w_attn_o, rel_bias, w_pool_g, pool_scale, w_pool_o, w_dw,
           b_dw, conv_ln_g, conv_ln_b, w_conv_o, w_out, post_mix_g, pre_ffn_g, w_ffn_in,
           w_ffn_out, post_ffn_g):
    b, s, d = x.shape
    depth = w_in.shape[0]
    attn_w = w_attn_o.shape[1]
    pool_w = w_pool_o.shape[1]
    conv_w = w_conv_o.shape[1]
    u_w = pool_w + 2 * conv_w

    row = lambda a: a.reshape(depth, 1, a.shape[-1])
    w_in_b = w_in.astype(BF16)
    wa_b, wp_b, wc_b, wo_b = (w.astype(BF16) for w in (w_attn_o, w_pool_o, w_conv_o, w_out))
    wfi_b, wfo_b = w_ffn_in.astype(BF16), w_ffn_out.astype(BF16)
    wg_bd = _block_diag(w_pool_g).astype(BF16)
    bias0_t, bias1_t = _bias_tables(rel_bias)

    x2 = x.reshape(b * s, d)
    for l in range(depth):
        qkv, u, hn = _inproj(x2, row(pre_mix_g), w_in_b, l, attn_w, u_w, conv_w)
        attn = _moba(qkv.reshape(b, s, 3 * attn_w), bias0_t, bias1_t, attn_w)
        x2 = _merge(x2, attn.reshape(b * s, attn_w), u, hn, s, w_in_b, row(b_gate),
                    wa_b, wp_b, wc_b, wo_b, row(post_mix_g), wg_bd, row(pool_scale), w_dw,
                    row(b_dw), row(conv_ln_g), row(conv_ln_b), l)
        x2 = _ffn(x2, row(pre_ffn_g), wfi_b, wfo_b, row(post_ffn_g), l)
    return x2.reshape(b, s, d)
```
